```python
import jax, jax.numpy as jnp
from jax import lax
import numpy as np

D_MODEL = 2048
BATCH = 2
SEQ = 8192
DEPTH = 4
DEC_BATCH = 32
DEC_SEQ = 32
PAST_LEN = 2048

CHUNK = 64
N_MIXERS = 3
N_A_LAYERS = (DEPTH + 2) // 3
N_B_LAYERS = (DEPTH + 1) // 3
N_C_LAYERS = DEPTH // 3
WINDOW = 128
WIN_CHUNKS = WINDOW // CHUNK
A_HEADS = 32
A_KV_HEADS = 4
A_HEAD_DIM = 64
A_GROUP = A_HEADS // A_KV_HEADS
ROPE_THETA = 10000.0
CONV_WIDTH = 31
GLA_HEADS = 4
GLA_DK = D_MODEL // 2
GLA_DV = D_MODEL
GLA_DK_HEAD = GLA_DK // GLA_HEADS
GLA_DV_HEAD = GLA_DV // GLA_HEADS
GLA_GATE_RANK = 16
GLA_TAU = 16.0
D_FF = 5632
FFN_CONV_WIDTH = 3
PLE_DIM = 256
EPS = 1e-6

kernel_name = "hybrid_streaming_encoder_step"


def rms_norm(x, g):
    xf = x.astype(jnp.float32)
    y = xf * lax.rsqrt(jnp.mean(xf * xf, axis=-1, keepdims=True) + EPS)
    return (y * g.astype(jnp.float32)).astype(x.dtype)


def layer_norm(x, g, b):
    xf = x.astype(jnp.float32)
    mu = jnp.mean(xf, axis=-1, keepdims=True)
    xc = xf - mu
    y = xc * lax.rsqrt(jnp.mean(xc * xc, axis=-1, keepdims=True) + EPS)
    return (y * g.astype(jnp.float32) + b.astype(jnp.float32)).astype(x.dtype)


def rope(x, pos):
    half = x.shape[-1] // 2
    inv = 1.0 / (ROPE_THETA ** (jnp.arange(half, dtype=jnp.float32) / half))
    ang = pos.astype(jnp.float32)[:, None] * inv[None, :]
    cos = jnp.cos(ang)[None, :, None, :]
    sin = jnp.sin(ang)[None, :, None, :]
    xf = x.astype(jnp.float32)
    x1, x2 = xf[..., :half], xf[..., half:]
    return jnp.concatenate([x1 * cos - x2 * sin, x2 * cos + x1 * sin], axis=-1).astype(x.dtype)


def causal_dwconv(x, buf, w, b):
    xp = jnp.concatenate([buf.astype(x.dtype), x], axis=1)
    y = lax.conv_general_dilated(xp, w.astype(x.dtype)[:, None, :], window_strides=(1,), padding='VALID',
                                 dimension_numbers=('NWC', 'WIO', 'NWC'), feature_group_count=x.shape[-1])
    return y + b.astype(x.dtype), xp[:, xp.shape[1] - (w.shape[0] - 1):]


def attn_qkv(xn, w_qkv, q_norm, k_norm, pos):
    B, L, _ = xn.shape
    q, k, v = jnp.split(xn @ w_qkv, [A_HEADS * A_HEAD_DIM, (A_HEADS + A_KV_HEADS) * A_HEAD_DIM], axis=-1)
    q = rope(rms_norm(q.reshape(B, L, A_HEADS, A_HEAD_DIM), q_norm), pos)
    k = rope(rms_norm(k.reshape(B, L, A_KV_HEADS, A_HEAD_DIM), k_norm), pos)
    v = v.reshape(B, L, A_KV_HEADS, A_HEAD_DIM)
    return q, k, v


def sink_softmax(scores, sinks, mask):
    s = sinks.astype(jnp.float32)[:, :, None, None]
    if mask is not None:
        scores = jnp.where(mask, scores, -jnp.inf)
    m = jnp.maximum(jnp.max(scores, axis=-1, keepdims=True), s)
    e = jnp.exp(scores - m)
    return e / (jnp.sum(e, axis=-1, keepdims=True) + jnp.exp(s - m))


def attn_prompt(xn, w_qkv, q_norm, k_norm, sinks, w_o):
    B, L, _ = xn.shape
    nC = L // CHUNK
    q, k, v = attn_qkv(xn, w_qkv, q_norm, k_norm, jnp.arange(L))
    qb = q.reshape(B, nC, CHUNK, A_KV_HEADS, A_GROUP, A_HEAD_DIM)
    pad = jnp.zeros((B, WIN_CHUNKS * CHUNK, A_KV_HEADS, A_HEAD_DIM), k.dtype)
    kp = jnp.concatenate([pad, k], axis=1).reshape(B, nC + WIN_CHUNKS, CHUNK, A_KV_HEADS, A_HEAD_DIM)
    vp = jnp.concatenate([pad.astype(v.dtype), v], axis=1).reshape(B, nC + WIN_CHUNKS, CHUNK, A_KV_HEADS, A_HEAD_DIM)
    kb = jnp.concatenate([kp[:, j:j + nC] for j in range(WIN_CHUNKS + 1)], axis=2)
    vb = jnp.concatenate([vp[:, j:j + nC] for j in range(WIN_CHUNKS + 1)], axis=2)
    key_chunk = jnp.arange(nC)[:, None] - WIN_CHUNKS + jnp.arange((WIN_CHUNKS + 1) * CHUNK)[None, :] // CHUNK
    mask = (key_chunk >= 0)[None, :, None, None, None, :]
    scores = jnp.einsum('bnqkgd,bnskd->bnkgqs', qb, kb, preferred_element_type=jnp.float32) * (A_HEAD_DIM ** -0.5)
    probs = sink_softmax(scores, sinks.reshape(A_KV_HEADS, A_GROUP), mask)
    o = jnp.einsum('bnkgqs,bnskd->bnqkgd', probs.astype(v.dtype), vb)
    y = o.reshape(B, L, A_HEADS * A_HEAD_DIM) @ w_o
    return y, k[:, L - WINDOW:], v[:, L - WINDOW:]


def attn_sample(xn, cache_k, cache_v, w_qkv, q_norm, k_norm, sinks, w_o):
    B, T, _ = xn.shape
    q, k, v = attn_qkv(xn, w_qkv, q_norm, k_norm, PAST_LEN + jnp.arange(T))
    kk = jnp.concatenate([cache_k.astype(k.dtype), k], axis=1)
    vv = jnp.concatenate([cache_v.astype(v.dtype), v], axis=1)
    qg = q.reshape(B, T, A_KV_HEADS, A_GROUP, A_HEAD_DIM)
    scores = jnp.einsum('bqkgd,bskd->bkgqs', qg, kk, preferred_element_type=jnp.float32) * (A_HEAD_DIM ** -0.5)
    probs = sink_softmax(scores, sinks.reshape(A_KV_HEADS, A_GROUP), None)
    o = jnp.einsum('bkgqs,bskd->bqkgd', probs.astype(vv.dtype), vv)
    y = o.reshape(B, T, A_HEADS * A_HEAD_DIM) @ w_o
    n = kk.shape[1]
    return y, kk[:, n - WINDOW:], vv[:, n - WINDOW:]


def conformer_conv(xn, buf, w_pw1, w_dw, b_dw, ln_g, ln_b, w_pw2):
    a, g = jnp.split(xn @ w_pw1, 2, axis=-1)
    u = a * jax.nn.sigmoid(g)
    c, new_buf = causal_dwconv(u, buf, w_dw, b_dw)
    c = layer_norm(c, ln_g, ln_b)
    return jax.nn.silu(c) @ w_pw2, new_buf


def gla_project(xn, w_in, w_gate_up, gate_bias):
    B, L, _ = xn.shape
    q, k, v, r, gl = jnp.split(xn @ w_in, [GLA_DK, 2 * GLA_DK, 2 * GLA_DK + GLA_DV, 2 * GLA_DK + 2 * GLA_DV], axis=-1)
    logit = (gl @ w_gate_up + gate_bias).astype(jnp.float32)
    log_a = jax.nn.log_sigmoid(logit) / GLA_TAU
    q = q.reshape(B, L, GLA_HEADS, GLA_DK_HEAD) * (GLA_DK_HEAD ** -0.5)
    k = k.reshape(B, L, GLA_HEADS, GLA_DK_HEAD)
    v = v.reshape(B, L, GLA_HEADS, GLA_DV_HEAD)
    log_a = log_a.reshape(B, L, GLA_HEADS, GLA_DK_HEAD)
    return q, k, v, log_a, r


def gla_chunk(S, q, k, v, log_a):
    qf, kf, vf = q.astype(jnp.float32), k.astype(jnp.float32), v.astype(jnp.float32)
    C = q.shape[1]
    b = jnp.cumsum(log_a, axis=1)
    qt = qf * jnp.exp(b)
    kt = kf * jnp.exp(-b)
    o_inter = jnp.einsum('bchk,bhkv->bchv', qt, S)
    A = jnp.einsum('bihk,bjhk->bhij', qt, kt)
    A = jnp.where(jnp.tril(jnp.ones((C, C), dtype=bool)), A, 0.0)
    o_intra = jnp.einsum('bhij,bjhv->bihv', A, vf)
    b_last = b[:, -1]
    k_dec = kf * jnp.exp(b_last[:, None] - b)
    S_new = jnp.exp(b_last)[..., None] * S + jnp.einsum('bchk,bchv->bhkv', k_dec, vf)
    return o_inter + o_intra, S_new


def gla_out(o, r, out_norm, w_o):
    B, L = o.shape[:2]
    o = rms_norm(o, out_norm).reshape(B, L, GLA_DV).astype(r.dtype)
    return (o * jax.nn.silu(r)) @ w_o


def gla_prompt(xn, w_in, w_gate_up, gate_bias, out_norm, w_o):
    B, L, _ = xn.shape
    nC = L // CHUNK
    q, k, v, log_a, r = gla_project(xn, w_in, w_gate_up, gate_bias)

    def to_chunks(t):
        return jnp.moveaxis(t.reshape(B, nC, CHUNK, *t.shape[2:]), 1, 0)

    def step(S, inp):
        o, S = gla_chunk(S, *inp)
        return S, o

    S0 = jnp.zeros((B, GLA_HEADS, GLA_DK_HEAD, GLA_DV_HEAD), jnp.float32)
    S, o = lax.scan(step, S0, (to_chunks(q), to_chunks(k), to_chunks(v), to_chunks(log_a)))
    o = jnp.moveaxis(o, 0, 1).reshape(B, L, GLA_HEADS, GLA_DV_HEAD)
    return gla_out(o, r, out_norm, w_o), S.astype(xn.dtype)


def gla_sample(xn, state, w_in, w_gate_up, gate_bias, out_norm, w_o):
    q, k, v, log_a, r = gla_project(xn, w_in, w_gate_up, gate_bias)
    o, S = gla_chunk(state.astype(jnp.float32), q, k, v, log_a)
    return gla_out(o, r, out_norm, w_o), S.astype(xn.dtype)


def conv_ffn(xn, buf, w_up, conv_w, conv_b, w_down):
    g, u = jnp.split(xn @ w_up, 2, axis=-1)
    gc, new_buf = causal_dwconv(g, buf, conv_w, conv_b)
    return (jax.nn.silu(gc) * u) @ w_down, new_buf


def per_layer_embed(h, p, w_proj, g, w_gate):
    return (p.astype(h.dtype) @ w_proj) * jax.nn.sigmoid(rms_norm(h, g) @ w_gate)


def setup_inputs(seed: int = 0) -> dict:
    key = jax.random.key(seed)
    ks = iter(list(jax.random.split(key, 48)))

    def nrm(shape, scale):
        return jax.random.normal(next(ks), shape, jnp.float32) * scale

    def gain(shape):
        return 1.0 + nrm(shape, 0.02)

    D = D_MODEL
    qkv_w = (A_HEADS + 2 * A_KV_HEADS) * A_HEAD_DIM
    gla_in = 2 * GLA_DK + 2 * GLA_DV + GLA_GATE_RANK
    return {
        "x_prompt": nrm((BATCH, SEQ, D), 1.0),
        "x_sample": nrm((DEC_BATCH, DEC_SEQ, D), 1.0),
        "p_prompt": nrm((DEPTH, BATCH, SEQ, PLE_DIM), 1.0),
        "p_sample": nrm((DEPTH, DEC_BATCH, DEC_SEQ, PLE_DIM), 1.0),
        "cache_k_a": nrm((N_A_LAYERS, DEC_BATCH, WINDOW, A_KV_HEADS, A_HEAD_DIM), 1.0),
        "cache_v_a": nrm((N_A_LAYERS, DEC_BATCH, WINDOW, A_KV_HEADS, A_HEAD_DIM), 1.0),
        "state_conv_b": nrm((N_B_LAYERS, DEC_BATCH, CONV_WIDTH - 1, D), 0.5),
        "state_gla_c": nrm((N_C_LAYERS, DEC_BATCH, GLA_HEADS, GLA_DK_HEAD, GLA_DV_HEAD), 1.0),
        "state_ffn_conv": nrm((DEPTH, DEC_BATCH, FFN_CONV_WIDTH - 1, D_FF), 1.0),
        "norm_mix": gain((DEPTH, D)),
        "norm_ffn": gain((DEPTH, D)),
        "a_w_qkv": nrm((N_A_LAYERS, D, qkv_w), D ** -0.5),
        "a_q_norm": gain((N_A_LAYERS, A_HEAD_DIM)),
        "a_k_norm": gain((N_A_LAYERS, A_HEAD_DIM)),
        "a_sinks": nrm((N_A_LAYERS, A_HEADS), 0.5),
        "a_w_o": nrm((N_A_LAYERS, A_HEADS * A_HEAD_DIM, D), (A_HEADS * A_HEAD_DIM) ** -0.5),
        "b_w_pw1": nrm((N_B_LAYERS, D, 2 * D), D ** -0.5),
        "b_w_dw": nrm((N_B_LAYERS, CONV_WIDTH, D), CONV_WIDTH ** -0.5),
        "b_dw_bias": nrm((N_B_LAYERS, D), 0.02),
        "b_ln_g": gain((N_B_LAYERS, D)),
        "b_ln_b": nrm((N_B_LAYERS, D), 0.02),
        "b_w_pw2": nrm((N_B_LAYERS, D, D), D ** -0.5),
        "c_w_in": nrm((N_C_LAYERS, D, gla_in), D ** -0.5),
        "c_w_gate_up": nrm((N_C_LAYERS, GLA_GATE_RANK, GLA_DK), GLA_GATE_RANK ** -0.5),
        "c_gate_bias": nrm((N_C_LAYERS, GLA_DK), 0.02),
        "c_out_norm": gain((N_C_LAYERS, GLA_DV_HEAD)),
        "c_w_o": nrm((N_C_LAYERS, GLA_DV, D), GLA_DV ** -0.5),
        "ffn_w_up": nrm((DEPTH, D, 2 * D_FF), D ** -0.5),
        "ffn_conv_w": nrm((DEPTH, FFN_CONV_WIDTH, D_FF), FFN_CONV_WIDTH ** -0.5),
        "ffn_conv_b": nrm((DEPTH, D_FF), 0.02),
        "ffn_w_down": nrm((DEPTH, D_FF, D), D_FF ** -0.5),
        "ple_w_proj": nrm((DEPTH, PLE_DIM, D), PLE_DIM ** -0.5),
        "ple_norm": gain((DEPTH, D)),
        "ple_w_gate": nrm((DEPTH, D, D), D ** -0.5),
    }


def reference(x_prompt, x_sample, p_prompt, p_sample, cache_k_a, cache_v_a, state_conv_b, state_gla_c,
              state_ffn_conv, norm_mix, norm_ffn, a_w_qkv, a_q_norm, a_k_norm, a_sinks, a_w_o,
              b_w_pw1, b_w_dw, b_dw_bias, b_ln_g, b_ln_b, b_w_pw2,
              c_w_in, c_w_gate_up, c_gate_bias, c_out_norm, c_w_o,
              ffn_w_up, ffn_conv_w, ffn_conv_b, ffn_w_down, ple_w_proj, ple_norm, ple_w_gate):
    hp, hs = x_prompt, x_sample
    Bp = hp.shape[0]
    kp_l, vp_l, ks_l, vs_l = [], [], [], []
    cbp_l, cbs_l, gp_l, gs_l = [], [], [], []
    fp_l, fs_l = [], []
    for i in range(DEPTH):
        kind, slot = i % N_MIXERS, i // N_MIXERS
        xpn = rms_norm(hp, norm_mix[i])
        xsn = rms_norm(hs, norm_mix[i])
        if kind == 0:
            mp, kp_, vp_ = attn_prompt(xpn, a_w_qkv[slot], a_q_norm[slot], a_k_norm[slot], a_sinks[slot], a_w_o[slot])
            ms, ks_, vs_ = attn_sample(xsn, cache_k_a[slot], cache_v_a[slot], a_w_qkv[slot], a_q_norm[slot],
                                       a_k_norm[slot], a_sinks[slot], a_w_o[slot])
            kp_l.append(kp_); vp_l.append(vp_); ks_l.append(ks_); vs_l.append(vs_)
        elif kind == 1:
            buf0 = jnp.zeros((Bp, CONV_WIDTH - 1, D_MODEL), hp.dtype)
            mp, cbp = conformer_conv(xpn, buf0, b_w_pw1[slot], b_w_dw[slot], b_dw_bias[slot], b_ln_g[slot],
                                     b_ln_b[slot], b_w_pw2[slot])
            ms, cbs = conformer_conv(xsn, state_conv_b[slot], b_w_pw1[slot], b_w_dw[slot], b_dw_bias[slot],
                                     b_ln_g[slot], b_ln_b[slot], b_w_pw2[slot])
            cbp_l.append(cbp); cbs_l.append(cbs)
        else:
            mp, gp = gla_prompt(xpn, c_w_in[slot], c_w_gate_up[slot], c_gate_bias[slot], c_out_norm[slot], c_w_o[slot])
            ms, gs = gla_sample(xsn, state_gla_c[slot], c_w_in[slot], c_w_gate_up[slot], c_gate_bias[slot],
                                c_out_norm[slot], c_w_o[slot])
            gp_l.append(gp); gs_l.append(gs)
        hp = hp + mp
        hs = hs + ms
        fbuf0 = jnp.zeros((Bp, FFN_CONV_WIDTH - 1, D_FF), hp.dtype)
        fp, fbp = conv_ffn(rms_norm(hp, norm_ffn[i]), fbuf0, ffn_w_up[i], ffn_conv_w[i], ffn_conv_b[i], ffn_w_down[i])
        fs, fbs = conv_ffn(rms_norm(hs, norm_ffn[i]), state_ffn_conv[i], ffn_w_up[i], ffn_conv_w[i], ffn_conv_b[i],
                           ffn_w_down[i])
        fp_l.append(fbp); fs_l.append(fbs)
        hp = hp + fp
        hs = hs + fs
        hp = hp + per_layer_embed(hp, p_prompt[i], ple_w_proj[i], ple_norm[i], ple_w_gate[i])
        hs = hs + per_layer_embed(hs, p_sample[i], ple_w_proj[i], ple_norm[i], ple_w_gate[i])
    new_k_a_prompt = jnp.stack(kp_l, axis=0)
    new_v_a_prompt = jnp.stack(vp_l, axis=0)
    new_conv_b_prompt = jnp.stack(cbp_l, axis=0)
    new_gla_c_prompt = jnp.stack(gp_l, axis=0)
    new_ffn_conv_prompt = jnp.stack(fp_l, axis=0)
    new_k_a_sample = jnp.stack(ks_l, axis=0)
    new_v_a_sample = jnp.stack(vs_l, axis=0)
    new_conv_b_sample = jnp.stack(cbs_l, axis=0)
    new_gla_c_sample = jnp.stack(gs_l, axis=0)
    new_ffn_conv_sample = jnp.stack(fs_l, axis=0)
    return (hp, hs, new_k_a_prompt, new_v_a_prompt, new_conv_b_prompt, new_gla_c_prompt, new_ffn_conv_prompt,
            new_k_a_sample, new_v_a_sample, new_conv_b_sample, new_gla_c_sample, new_ffn_conv_sample)
```

```python
import functools

import jax
import jax.numpy as jnp
from jax import lax
from jax.experimental import pallas as pl
from jax.experimental.pallas import tpu as pltpu

F32 = jnp.float32
BF16 = jnp.bfloat16

EPS = 1e-6
CHUNK = 64
WINDOW = 128
A_HEAD_DIM = 64
A_KV_HEADS = 4
ROPE_THETA = 10000.0
PAST_LEN = 2048
GLA_HEADS = 4
GLA_TAU = 16.0
LANES = 128
VMEM_LIMIT_BYTES = 56 * 2**20


def _params(n_axes):
    return pltpu.CompilerParams(dimension_semantics=("arbitrary",) * n_axes,
                                vmem_limit_bytes=VMEM_LIMIT_BYTES)


def _sigmoid(x):
    return 1.0 / (1.0 + jnp.exp(-x))


def _silu(x):
    return x * _sigmoid(x)


def _rms_to_bf16(x, g):
    ms = jnp.mean(x * x, axis=-1, keepdims=True)
    return ((x * lax.rsqrt(ms + EPS)) * g).astype(BF16)


def _row_tile(m, want):
    t = min(m, want)
    assert m % t == 0, (m, t)
    return t


def _linear_body(*refs, has_norm, n_w, n_extra, epilogue):
    it = iter(refs)
    x_ref = next(it)
    g_ref = next(it) if has_norm else None
    w_refs = [next(it) for _ in range(n_w)]
    extra_refs = [next(it) for _ in range(n_extra)]
    o_ref = next(it)
    if has_norm:
        xn_ref = next(it)

        @pl.when(pl.program_id(1) == 0)
        def _():
            xn_ref[...] = _rms_to_bf16(x_ref[...], g_ref[...])

        lhs = xn_ref[...]
    else:
        lhs = x_ref[...].astype(BF16)
    accs = [jnp.dot(lhs, w[...], preferred_element_type=F32) for w in w_refs]
    o_ref[...] = epilogue(accs, extra_refs).astype(o_ref.dtype)


def _linear(x, w, *, n_out, tm, tn, epilogue, out_dtype, name, gain=None,
            w_offsets=(0,), extras=()):
    m, k = x.shape
    tm = _row_tile(m, tm)
    assert n_out % tn == 0
    in_specs = [pl.BlockSpec((tm, k), lambda i, j: (i, 0))]
    args = [x]
    if gain is not None:
        in_specs.append(pl.BlockSpec((1, k), lambda i, j: (0, 0)))
        args.append(gain.reshape(1, k))
    for off in w_offsets:
        in_specs.append(pl.BlockSpec((k, tn), lambda i, j, off=off: (0, j + off)))
        args.append(w)
    for arr, shape, imap in extras:
        in_specs.append(pl.BlockSpec(shape, imap))
        args.append(arr)
    body = functools.partial(_linear_body, has_norm=gain is not None, n_w=len(w_offsets),
                             n_extra=len(extras), epilogue=epilogue)
    return pl.pallas_call(
        body,
        grid=(m // tm, n_out // tn),
        in_specs=in_specs,
        out_specs=pl.BlockSpec((tm, tn), lambda i, j: (i, j)),
        out_shape=jax.ShapeDtypeStruct((m, n_out), out_dtype),
        scratch_shapes=[pltpu.VMEM((tm, k), BF16)] if gain is not None else [],
        compiler_params=_params(2),
        name=name,
    )(*args)


def _ep_plain(accs, extras):
    return accs[0]


def _ep_residual(accs, extras):
    return extras[0][...] + accs[0]


def _ep_glu(accs, extras):
    return accs[0] * _sigmoid(accs[1])


def _ep_ple(accs, extras):
    res_ref, p_ref, wp_ref = extras
    proj = jnp.dot(p_ref[...].astype(BF16), wp_ref[...], preferred_element_type=F32)
    return res_ref[...] + proj * _sigmoid(accs[0])


def _residual_extra(h, tm, tn):
    return (h, (_row_tile(h.shape[0], tm), tn), lambda i, j: (i, j))


def _rope_tables(pos):
    half = A_HEAD_DIM // 2
    inv = 1.0 / (ROPE_THETA ** (jnp.arange(half, dtype=F32) / half))
    ang = pos.astype(F32)[:, None] * inv[None, :]
    cos, sin = jnp.cos(ang), jnp.sin(ang)
    return jnp.tile(cos, (1, 4)), jnp.tile(jnp.concatenate([-sin, sin], axis=1), (1, 2))


def _attn_prep_body(qkv_ref, cos_ref, sin_ref, qg_ref, kg_ref, q_ref, k_ref, *, n_q_pairs,
                    n_k_pairs):
    cos = cos_ref[...]
    sin = sin_ref[...]
    lane = lax.broadcasted_iota(jnp.int32, cos.shape, 1)
    left = lane < A_HEAD_DIM
    first_half = (lane & (A_HEAD_DIM // 2)) == 0
    for p in range(n_q_pairs + n_k_pairs):
        is_q = p < n_q_pairs
        x = qkv_ref[:, p * LANES:(p + 1) * LANES]
        sq = x * x
        s_left = jnp.sum(jnp.where(left, sq, 0.0), axis=-1, keepdims=True)
        s_right = jnp.sum(jnp.where(left, 0.0, sq), axis=-1, keepdims=True)
        ms = jnp.where(left, s_left, s_right) * (1.0 / A_HEAD_DIM)
        y = (x * lax.rsqrt(ms + EPS)) * (qg_ref[...] if is_q else kg_ref[...])
        partner = jnp.where(first_half, pltpu.roll(y, LANES - A_HEAD_DIM // 2, 1),
                            pltpu.roll(y, A_HEAD_DIM // 2, 1))
        out = y * cos + partner * sin
        if is_q:
            q_ref[:, p * LANES:(p + 1) * LANES] = (out * (A_HEAD_DIM ** -0.5)).astype(BF16)
        else:
            pk = p - n_q_pairs
            k_ref[:, pk * LANES:(pk + 1) * LANES] = out


def _attn_prep(qkv, cos, sin, q_norm, k_norm, *, seq_len, n_heads):
    m, width = qkv.shape
    tm = _row_tile(m, 256)
    if seq_len < tm:
        cos = jnp.tile(cos, (tm // seq_len, 1))
        sin = jnp.tile(sin, (tm // seq_len, 1))
    n_tab = cos.shape[0] // tm
    dq = n_heads * A_HEAD_DIM
    dk = A_KV_HEADS * A_HEAD_DIM
    body = functools.partial(_attn_prep_body, n_q_pairs=dq // LANES, n_k_pairs=dk // LANES)
    tab_spec = pl.BlockSpec((tm, LANES), lambda i: (i % n_tab, 0))
    gain_spec = pl.BlockSpec((1, LANES), lambda i: (0, 0))
    return pl.pallas_call(
        body,
        grid=(m // tm,),
        in_specs=[pl.BlockSpec((tm, width), lambda i: (i, 0)), tab_spec, tab_spec, gain_spec,
                  gain_spec],
        out_specs=[pl.BlockSpec((tm, dq), lambda i: (i, 0)),
                   pl.BlockSpec((tm, dk), lambda i: (i, 0))],
        out_shape=[jax.ShapeDtypeStruct((m, dq), BF16), jax.ShapeDtypeStruct((m, dk), F32)],
        compiler_params=_params(1),
        name="attn_prep",
    )(qkv, cos, sin, jnp.tile(q_norm, 2).reshape(1, LANES), jnp.tile(k_norm, 2).reshape(1, LANES))


def _attn_heads(q_ref, k_all, v_all, sinks_ref, o_ref, mask, n_heads):
    group = n_heads // A_KV_HEADS
    for g in range(A_KV_HEADS):
        kg = k_all[:, g * A_HEAD_DIM:(g + 1) * A_HEAD_DIM]
        vg = v_all[:, g * A_HEAD_DIM:(g + 1) * A_HEAD_DIM]
        for h in range(g * group, (g + 1) * group):
            qh = q_ref[:, h * A_HEAD_DIM:(h + 1) * A_HEAD_DIM]
            s = lax.dot_general(qh, kg, (((1,), (1,)), ((), ())), preferred_element_type=F32)
            if mask is not None:
                s = jnp.where(mask, s, -jnp.inf)
            sink = sinks_ref[h]
            mx = jnp.maximum(jnp.max(s, axis=-1, keepdims=True), sink)
            e = jnp.exp(s - mx)
            denom = jnp.sum(e, axis=-1, keepdims=True) + jnp.exp(sink - mx)
            p = (e * (1.0 / denom)).astype(BF16)
            o = jnp.dot(p, vg, preferred_element_type=F32)
            o_ref[:, h * A_HEAD_DIM:(h + 1) * A_HEAD_DIM] = o.astype(o_ref.dtype)


def _attn_prompt_body(sinks_ref, q_ref, kp_ref, kc_ref, vp_ref, vc_ref, o_ref, *, n_heads):
    tq = q_ref.shape[0]
    i = pl.program_id(1)
    k_all = jnp.concatenate([kp_ref[...], kc_ref[...]], axis=0).astype(BF16)
    v_all = jnp.concatenate([vp_ref[...], vc_ref[...]], axis=0).astype(BF16)
    n_keys = WINDOW + tq
    q_chunk = lax.broadcasted_iota(jnp.int32, (tq, n_keys), 0) // CHUNK
    k_chunk = lax.broadcasted_iota(jnp.int32, (tq, n_keys), 1) // CHUNK - WINDOW // CHUNK
    mask = (k_chunk <= q_chunk) & (k_chunk >= q_chunk - WINDOW // CHUNK)
    mask = mask & ((k_chunk >= 0) | (i > 0))
    _attn_heads(q_ref, k_all, v_all, sinks_ref, o_ref, mask, n_heads)


def _attn_prompt(q, k, qkv, sinks, *, batch, seq_len, n_heads):
    m = q.shape[0]
    tq = _row_tile(seq_len, 256)
    nq = seq_len // tq
    per_w = tq // WINDOW
    v_col = (qkv.shape[1] - A_KV_HEADS * A_HEAD_DIM) // (A_KV_HEADS * A_HEAD_DIM)
    dk = A_KV_HEADS * A_HEAD_DIM

    def prev_row(b, i):
        return jnp.maximum((b * nq + i) * per_w - 1, 0)

    return pl.pallas_call(
        functools.partial(_attn_prompt_body, n_heads=n_heads),
        grid=(batch, nq),
        in_specs=[
            pl.BlockSpec(memory_space=pltpu.SMEM),
            pl.BlockSpec((tq, q.shape[1]), lambda b, i: (b * nq + i, 0)),
            pl.BlockSpec((WINDOW, dk), lambda b, i: (prev_row(b, i), 0)),
            pl.BlockSpec((tq, dk), lambda b, i: (b * nq + i, 0)),
            pl.BlockSpec((WINDOW, dk), lambda b, i: (prev_row(b, i), v_col)),
            pl.BlockSpec((tq, dk), lambda b, i: (b * nq + i, v_col)),
        ],
        out_specs=pl.BlockSpec((tq, q.shape[1]), lambda b, i: (b * nq + i, 0)),
        out_shape=jax.ShapeDtypeStruct((m, q.shape[1]), BF16),
        compiler_params=_params(2),
        name="attn_prompt",
    )(sinks, q, k, k, qkv, qkv)


def _attn_sample_body(sinks_ref, q_ref, kn_ref, vn_ref, ck_ref, cv_ref, o_ref, nk_ref, nv_ref, *,
                      n_heads):
    t = q_ref.shape[0]
    k_f32 = jnp.concatenate([ck_ref[0], kn_ref[...]], axis=0)
    v_f32 = jnp.concatenate([cv_ref[0], vn_ref[...]], axis=0)
    nk_ref[0] = k_f32[t:]
    nv_ref[0] = v_f32[t:]
    _attn_heads(q_ref, k_f32.astype(BF16), v_f32.astype(BF16), sinks_ref, o_ref, None, n_heads)


def _attn_sample(q, k, qkv, cache_k, cache_v, sinks, *, batch, seq_len, n_heads):
    m = q.shape[0]
    dk = A_KV_HEADS * A_HEAD_DIM
    v_col = (qkv.shape[1] - dk) // dk
    cache_spec = pl.BlockSpec((1, WINDOW, dk), lambda b: (b, 0, 0))
    return pl.pallas_call(
        functools.partial(_attn_sample_body, n_heads=n_heads),
        grid=(batch,),
        in_specs=[
            pl.BlockSpec(memory_space=pltpu.SMEM),
            pl.BlockSpec((seq_len, q.shape[1]), lambda b: (b, 0)),
            pl.BlockSpec((seq_len, dk), lambda b: (b, 0)),
            pl.BlockSpec((seq_len, dk), lambda b: (b, v_col)),
            cache_spec, cache_spec,
        ],
        out_specs=[pl.BlockSpec((seq_len, q.shape[1]), lambda b: (b, 0)), cache_spec, cache_spec],
        out_shape=[jax.ShapeDtypeStruct((m, q.shape[1]), BF16),
                   jax.ShapeDtypeStruct((batch, WINDOW, dk), F32),
                   jax.ShapeDtypeStruct((batch, WINDOW, dk), F32)],
        compiler_params=_params(1),
        name="attn_sample",
    )(sinks, q, k, qkv, cache_k.reshape(batch, WINDOW, dk), cache_v.reshape(batch, WINDOW, dk))


CONV_HIST = 32
CONV_ROWS = 64
CONV_COLS = 256


def _conv_ln_body(u_ref, st_ref, w_ref, b_ref, g_ref, beta_ref, o_ref, ext_ref, c_ref, *, width):
    tm, d = u_ref.shape
    pad = CONV_HIST - (width - 1)

    @pl.when(pl.program_id(1) == 0)
    def _():
        ext_ref[0:CONV_HIST, :] = st_ref[0]

    ext_ref[CONV_HIST:CONV_HIST + tm, :] = u_ref[...]
    rows = min(CONV_ROWS, tm)
    for cs in range(0, d, CONV_COLS):
        for rb in range(0, tm, rows):
            acc = jnp.broadcast_to(b_ref[:, cs:cs + CONV_COLS], (rows, CONV_COLS))
            for k in range(width):
                tap = ext_ref[rb + pad + k:rb + pad + k + rows, cs:cs + CONV_COLS]
                acc = acc + tap * w_ref[k:k + 1, cs:cs + CONV_COLS]
            c_ref[rb:rb + rows, cs:cs + CONV_COLS] = acc
    carry = ext_ref[tm:tm + CONV_HIST, :]
    ext_ref[0:CONV_HIST, :] = carry
    for rb in range(0, tm, rows):
        c = c_ref[rb:rb + rows, :]
        mu = jnp.mean(c, axis=-1, keepdims=True)
        xc = c - mu
        var = jnp.mean(xc * xc, axis=-1, keepdims=True)
        y = (xc * lax.rsqrt(var + EPS)) * g_ref[...] + beta_ref[...]
        o_ref[rb:rb + rows, :] = _silu(y).astype(o_ref.dtype)


def _conv_ln(u, state, w_dw, b_dw, ln_g, ln_b, *, batch, seq_len):
    m, d = u.shape
    width = w_dw.shape[0]
    tm = _row_tile(seq_len, 128)
    nt = seq_len // tm
    st = jnp.pad(state, ((0, 0), (CONV_HIST - (width - 1), 0), (0, 0)))
    w = jnp.pad(w_dw, ((0, CONV_HIST - width), (0, 0)))
    row = lambda a: a.reshape(1, d)
    vec_spec = pl.BlockSpec((1, d), lambda b, t: (0, 0))
    return pl.pallas_call(
        functools.partial(_conv_ln_body, width=width),
        grid=(batch, nt),
        in_specs=[pl.BlockSpec((tm, d), lambda b, t: (b * nt + t, 0)),
                  pl.BlockSpec((1, CONV_HIST, d), lambda b, t: (b, 0, 0)),
                  pl.BlockSpec((CONV_HIST, d), lambda b, t: (0, 0)),
                  vec_spec, vec_spec, vec_spec],
        out_specs=pl.BlockSpec((tm, d), lambda b, t: (b * nt + t, 0)),
        out_shape=jax.ShapeDtypeStruct((m, d), BF16),
        scratch_shapes=[pltpu.VMEM((CONV_HIST + tm, d), F32), pltpu.VMEM((tm, d), F32)],
        compiler_params=_params(2),
        name="conv_ln",
    )(u, st, w, row(b_dw), row(ln_g), row(ln_b))


def _split3(x):
    h1 = x.astype(BF16)
    r1 = x - h1.astype(F32)
    h2 = r1.astype(BF16)
    h3 = (r1 - h2.astype(F32)).astype(BF16)
    return h1, h2, h3


def _exact_dot(dims, a_f32, b_bf16):
    return sum(lax.dot_general(piece, b_bf16, dims, preferred_element_type=F32)
               for piece in _split3(a_f32))


def _gla_body(proj_ref, wgu_ref, bias_ref, norm_ref, s0_ref, o_ref, sout_ref, s_ref, *, dk, dv):
    c = proj_ref.shape[0]
    dkh, dvh = dk // GLA_HEADS, dv // GLA_HEADS
    step = pl.program_id(1)

    @pl.when(step == 0)
    def _():
        s_ref[...] = s0_ref[0]

    gl = proj_ref[:, 2 * dk + 2 * dv:].astype(BF16)
    logit = jnp.dot(gl, wgu_ref[...], preferred_element_type=F32) + bias_ref[...]
    log_a = (jnp.minimum(logit, 0.0) - jnp.log1p(jnp.exp(-jnp.abs(logit)))) * (1.0 / GLA_TAU)
    ri = lax.broadcasted_iota(jnp.int32, (c, c), 0)
    ci = lax.broadcasted_iota(jnp.int32, (c, c), 1)
    causal = ri >= ci
    tri = jnp.where(causal, 1.0, 0.0).astype(BF16)
    b_all = sum(jnp.dot(tri, piece, preferred_element_type=F32) for piece in _split3(log_a))
    last_sel = jnp.where(lax.broadcasted_iota(jnp.int32, (c, LANES), 0) == c - 1, 1.0, 0.0).astype(BF16)
    tn_dims = (((0,), (0,)), ((), ()))
    for h in range(GLA_HEADS):
        b = b_all[:, h * dkh:(h + 1) * dkh]
        b_last = b[c - 1:c, :]
        q = proj_ref[:, h * dkh:(h + 1) * dkh] * (dkh ** -0.5)
        k = proj_ref[:, dk + h * dkh:dk + (h + 1) * dkh]
        v = proj_ref[:, 2 * dk + h * dvh:2 * dk + (h + 1) * dvh].astype(BF16)
        r = proj_ref[:, 2 * dk + dv + h * dvh:2 * dk + dv + (h + 1) * dvh]
        qt = (q * jnp.exp(b)).astype(BF16)
        kt = (k * jnp.exp(-b)).astype(BF16)
        k_dec = (k * jnp.exp(b_last - b)).astype(BF16)
        s_old = s_ref[h]
        o_inter = jnp.dot(qt, s_old.astype(BF16), preferred_element_type=F32)
        a = lax.dot_general(qt, kt, (((1,), (1,)), ((), ())), preferred_element_type=F32)
        a = jnp.where(causal, a, 0.0).astype(BF16)
        o = o_inter + jnp.dot(a, v, preferred_element_type=F32)
        b_last_col = _exact_dot(tn_dims, b, last_sel)
        decay = jnp.exp(b_last_col[:, 0:1])
        s_ref[h] = decay * s_old + lax.dot_general(k_dec, v, tn_dims, preferred_element_type=F32)
        ms = jnp.mean(o * o, axis=-1, keepdims=True)
        o_n = (o * lax.rsqrt(ms + EPS)) * norm_ref[...]
        o_ref[:, h * dvh:(h + 1) * dvh] = (o_n * _silu(r)).astype(o_ref.dtype)

    @pl.when(step == pl.num_programs(1) - 1)
    def _():
        sout_ref[0] = s_ref[...]


def _gla(proj, wgu, bias, out_norm, state, *, batch, seq_len, dk, dv):
    m, width = proj.shape
    c = min(CHUNK, seq_len)
    nc = seq_len // c
    dkh, dvh = dk // GLA_HEADS, dv // GLA_HEADS
    rank = wgu.shape[0]
    wgu_pad = jnp.pad(wgu, ((0, width - 2 * dk - 2 * dv - rank), (0, 0))).astype(BF16)
    state_spec = pl.BlockSpec((1, GLA_HEADS, dkh, dvh), lambda b, t: (b, 0, 0, 0))
    return pl.pallas_call(
        functools.partial(_gla_body, dk=dk, dv=dv),
        grid=(batch, nc),
        in_specs=[pl.BlockSpec((c, width), lambda b, t: (b * nc + t, 0)),
                  pl.BlockSpec(wgu_pad.shape, lambda b, t: (0, 0)),
                  pl.BlockSpec((1, dk), lambda b, t: (0, 0)),
                  pl.BlockSpec((1, dvh), lambda b, t: (0, 0)),
                  state_spec],
        out_specs=[pl.BlockSpec((c, dv), lambda b, t: (b * nc + t, 0)), state_spec],
        out_shape=[jax.ShapeDtypeStruct((m, dv), BF16),
                   jax.ShapeDtypeStruct((batch, GLA_HEADS, dkh, dvh), F32)],
        scratch_shapes=[pltpu.VMEM((GLA_HEADS, dkh, dvh), F32)],
        compiler_params=_params(2),
        name="gla",
    )(proj, wgu_pad, bias.reshape(1, dk), out_norm.reshape(1, dvh), state)


def _ffn_body(x_ref, gain_ref, wg_ref, wu_ref, cw_ref, cb_ref, wd_ref, st_ref, o_ref, ns_ref,
              xn_ref, *carry, seq_tile, tiles_per_seq):
    tm = x_ref.shape[0]
    tn = wg_ref.shape[1]
    nseq = tm // seq_tile
    i = pl.program_id(0)
    j = pl.program_id(1)

    @pl.when(j == 0)
    def _():
        x = x_ref[...]
        xn_ref[...] = _rms_to_bf16(x, gain_ref[...])
        o_ref[...] = x

    xn = xn_ref[...]
    g = jnp.dot(xn, wg_ref[...], preferred_element_type=F32)
    u = jnp.dot(xn, wu_ref[...], preferred_element_type=F32)
    if tiles_per_seq > 1:
        prev = jnp.where(i % tiles_per_seq == 0, st_ref[...], carry[0][j])
    else:
        prev = st_ref[...]
    g3 = g.reshape(nseq, seq_tile, tn)
    row = lax.broadcasted_iota(jnp.int32, (nseq, seq_tile, tn), 1)
    back1 = jnp.where(row == 0, prev[:, 1:2, :], pltpu.roll(g, 1, 0).reshape(nseq, seq_tile, tn))
    back2 = jnp.where(row == 0, prev[:, 0:1, :],
                      jnp.where(row == 1, prev[:, 1:2, :],
                                pltpu.roll(g, 2, 0).reshape(nseq, seq_tile, tn)))
    gc = (back2 * cw_ref[0:1, :] + back1 * cw_ref[1:2, :] + g3 * cw_ref[2:3, :]) + cb_ref[...]
    act = (_silu(gc).reshape(tm, tn) * u).astype(BF16)
    o_ref[...] += jnp.dot(act, wd_ref[...], preferred_element_type=F32)
    tail = g3[:, seq_tile - 2:seq_tile, :]
    ns_ref[...] = tail
    if tiles_per_seq > 1:
        carry[0][j] = tail


def _ffn(h, gain, w_up, conv_w, conv_b, w_down, state, *, seq_len, tn=512):
    m, d = h.shape
    dff = w_down.shape[0]
    assert dff % tn == 0
    nj = dff // tn
    tm = _row_tile(m, 512)
    seq_tile = min(seq_len, tm)
    tiles_per_seq = seq_len // seq_tile
    nseq = tm // seq_tile
    width = conv_w.shape[0]
    assert width == 3 and state.shape[1] == width - 1
    st_map = lambda i, j: (i // tiles_per_seq, 0, j)
    tail_map = lambda i, j: (i, 0, j)
    scratch = [pltpu.VMEM((tm, d), BF16)]
    if tiles_per_seq > 1:
        scratch.append(pltpu.VMEM((nj, nseq, width - 1, tn), F32))
    out, tails = pl.pallas_call(
        functools.partial(_ffn_body, seq_tile=seq_tile, tiles_per_seq=tiles_per_seq),
        grid=(m // tm, nj),
        in_specs=[pl.BlockSpec((tm, d), lambda i, j: (i, 0)),
                  pl.BlockSpec((1, d), lambda i, j: (0, 0)),
                  pl.BlockSpec((d, tn), lambda i, j: (0, j)),
                  pl.BlockSpec((d, tn), lambda i, j: (0, j + nj)),
                  pl.BlockSpec((width, tn), lambda i, j: (0, j)),
                  pl.BlockSpec((1, tn), lambda i, j: (0, j)),
                  pl.BlockSpec((tn, d), lambda i, j: (j, 0)),
                  pl.BlockSpec((nseq, width - 1, tn), st_map)],
        out_specs=[pl.BlockSpec((tm, d), lambda i, j: (i, 0)),
                   pl.BlockSpec((nseq, width - 1, tn), tail_map)],
        out_shape=[jax.ShapeDtypeStruct((m, d), F32),
                   jax.ShapeDtypeStruct((m // seq_tile, width - 1, dff), F32)],
        scratch_shapes=scratch,
        compiler_params=_params(2),
        name="conv_ffn",
    )(h, gain.reshape(1, d), w_up, w_up, conv_w, conv_b.reshape(1, dff), w_down, state)
    return out, tails.reshape(-1, tiles_per_seq, width - 1, dff)[:, -1]


def _run_group(h, p, *, batch, seq_len, first_pos, states, weights):
    (cache_k, cache_v, st_conv, st_gla, st_ffn) = states
    w = weights
    m, d = h.shape
    depth = w["norm_mix"].shape[0]
    n_heads = w["a_w_o"].shape[1] // A_HEAD_DIM
    dkv = A_KV_HEADS * A_HEAD_DIM
    cos, sin = _rope_tables(first_pos + jnp.arange(seq_len))
    new_k, new_v, new_conv, new_gla, new_ffn = [], [], [], [], []
    for layer in range(depth):
        kind, slot = layer % 3, layer // 3
        if kind == 0:
            wqkv = w["a_w_qkv"][slot]
            n_qkv = wqkv.shape[1]
            qkv = _linear(h, wqkv, n_out=n_qkv, tm=512, tn=n_qkv // 2, epilogue=_ep_plain,
                          out_dtype=F32, gain=w["norm_mix"][layer], name="attn_qkv")
            q, k = _attn_prep(qkv, cos, sin, w["a_q_norm"][slot], w["a_k_norm"][slot],
                              seq_len=seq_len, n_heads=n_heads)
            if cache_k is None:
                o = _attn_prompt(q, k, qkv, w["a_sinks"][slot], batch=batch, seq_len=seq_len,
                                 n_heads=n_heads)
                new_k.append(k.reshape(batch, seq_len, dkv)[:, seq_len - WINDOW:])
                new_v.append(qkv.reshape(batch, seq_len, n_qkv)[:, seq_len - WINDOW:, n_qkv - dkv:])
            else:
                o, nk, nv = _attn_sample(q, k, qkv, cache_k[slot], cache_v[slot], w["a_sinks"][slot],
                                         batch=batch, seq_len=seq_len, n_heads=n_heads)
                new_k.append(nk)
                new_v.append(nv)
            mix_in, mix_w = o, w["a_w_o"][slot]
        elif kind == 1:
            tn = 1024
            u = _linear(h, w["b_w_pw1"][slot], n_out=d, tm=512, tn=tn, epilogue=_ep_glu,
                        out_dtype=F32, gain=w["norm_mix"][layer], w_offsets=(0, d // tn),
                        name="conf_pw1_glu")
            width = w["b_w_dw"].shape[1]
            if st_conv is None:
                conv_state = jnp.zeros((batch, width - 1, d), F32)
            else:
                conv_state = st_conv[slot]
            mix_in = _conv_ln(u, conv_state, w["b_w_dw"][slot], w["b_dw_bias"][slot],
                              w["b_ln_g"][slot], w["b_ln_b"][slot], batch=batch, seq_len=seq_len)
            xp = jnp.concatenate([conv_state, u.reshape(batch, seq_len, d)], axis=1)
            new_conv.append(xp[:, seq_len:])
            mix_w = w["b_w_pw2"][slot]
        else:
            w_in = w["c_w_in"][slot]
            dk = w["c_w_gate_up"].shape[2]
            dv = w["c_w_o"].shape[1]
            proj = _linear(h, w_in, n_out=w_in.shape[1], tm=512, tn=w_in.shape[1] // 7,
                           epilogue=_ep_plain, out_dtype=F32, gain=w["norm_mix"][layer],
                           name="gla_in")
            if st_gla is None:
                s0 = jnp.zeros((batch, GLA_HEADS, dk // GLA_HEADS, dv // GLA_HEADS), F32)
            else:
                s0 = st_gla[slot]
            mix_in, s_new = _gla(proj, w["c_w_gate_up"][slot], w["c_gate_bias"][slot],
                                 w["c_out_norm"][slot], s0, batch=batch, seq_len=seq_len, dk=dk, dv=dv)
            new_gla.append(s_new)
            mix_w = w["c_w_o"][slot]
        h = _linear(mix_in, mix_w, n_out=d, tm=512, tn=d, epilogue=_ep_residual, out_dtype=F32,
                    extras=[_residual_extra(h, 512, d)], name="mixer_out")
        dff = w["ffn_w_down"].shape[1]
        if st_ffn is None:
            ffn_state = jnp.zeros((batch, w["ffn_conv_w"].shape[1] - 1, dff), F32)
        else:
            ffn_state = st_ffn[layer]
        h, ns = _ffn(h, w["norm_ffn"][layer], w["ffn_w_up"][layer], w["ffn_conv_w"][layer],
                     w["ffn_conv_b"][layer], w["ffn_w_down"][layer], ffn_state, seq_len=seq_len)
        new_ffn.append(ns)
        tm = _row_tile(m, 512)
        ple_dim = p.shape[-1]
        h = _linear(h, w["ple_w_gate"][layer], n_out=d, tm=512, tn=d, epilogue=_ep_ple,
                    out_dtype=F32, gain=w["ple_norm"][layer],
                    extras=[(h, (tm, d), lambda i, j: (i, j)),
                            (p[layer], (tm, ple_dim), lambda i, j: (i, 0)),
                            (w["ple_w_proj"][layer], (ple_dim, d), lambda i, j: (0, j))],
                    name="ple")
    kv_shape = (len(new_k), batch, WINDOW, A_KV_HEADS, A_HEAD_DIM)
    return (h.reshape(batch, seq_len, d),
            jnp.stack(new_k).reshape(kv_shape), jnp.stack(new_v).reshape(kv_shape),
            jnp.stack(new_conv), jnp.stack(new_gla), jnp.stack(new_ffn))


def kernel(x_prompt, x_sample, p_prompt, p_sample, cache_k_a, cache_v_a, state_conv_b, state_gla_c, state_ffn_conv, norm_mix, norm_ffn, a_w_qkv, a_q_norm, a_k_norm, a_sinks, a_w_o, b_w_pw1, b_w_dw, b_dw_bias, b_ln_g, b_ln_b, b_w_pw2, c_w_in, c_w_gate_up, c_gate_bias, c_out_norm, c_w_o, ffn_w_up, ffn_conv_w, ffn_conv_b, ffn_w_down, ple_w_proj, ple_norm, ple_w_gate):
    bp, lp, d = x_prompt.shape
    bs, ls, _ = x_sample.shape
    depth = norm_mix.shape[0]
    gla_cols = c_w_in.shape[2]
    gla_pad = (-gla_cols) % (7 * LANES)
    weights = dict(
        norm_mix=norm_mix, norm_ffn=norm_ffn,
        a_w_qkv=a_w_qkv.astype(BF16), a_q_norm=a_q_norm, a_k_norm=a_k_norm, a_sinks=a_sinks,
        a_w_o=a_w_o.astype(BF16),
        b_w_pw1=b_w_pw1.astype(BF16), b_w_dw=b_w_dw, b_dw_bias=b_dw_bias, b_ln_g=b_ln_g,
        b_ln_b=b_ln_b, b_w_pw2=b_w_pw2.astype(BF16),
        c_w_in=jnp.pad(c_w_in, ((0, 0), (0, 0), (0, gla_pad))).astype(BF16),
        c_w_gate_up=c_w_gate_up, c_gate_bias=c_gate_bias, c_out_norm=c_out_norm,
        c_w_o=c_w_o.astype(BF16),
        ffn_w_up=ffn_w_up.astype(BF16), ffn_conv_w=ffn_conv_w, ffn_conv_b=ffn_conv_b,
        ffn_w_down=ffn_w_down.astype(BF16),
        ple_w_proj=ple_w_proj.astype(BF16), ple_norm=ple_norm, ple_w_gate=ple_w_gate.astype(BF16),
    )
    prompt = _run_group(x_prompt.reshape(bp * lp, d), p_prompt.reshape(depth, bp * lp, -1),
                        batch=bp, seq_len=lp, first_pos=0,
                        states=(None, None, None, None, None), weights=weights)
    sample = _run_group(x_sample.reshape(bs * ls, d), p_sample.reshape(depth, bs * ls, -1),
                        batch=bs, seq_len=ls, first_pos=PAST_LEN,
                        states=(cache_k_a, cache_v_a, state_conv_b, state_gla_c, state_ffn_conv),
                        weights=weights)
    return (prompt[0], sample[0]) + prompt[1:] + sample[1:]
```

```python
import functools

import jax
import jax.numpy as jnp
from jax import lax
from jax.experimental import pallas as pl
from jax.experimental.pallas import tpu as pltpu

F32 = jnp.float32
BF16 = jnp.bfloat16

EPS = 1e-6
CHUNK = 64
WINDOW = 128
A_HEAD_DIM = 64
A_KV_HEADS = 4
ROPE_THETA = 10000.0
PAST_LEN = 2048
GLA_HEADS = 4
GLA_TAU = 16.0
GLA_IN_TN = 1280
LANES = 128
SUBLANES = 8
VMEM_LIMIT_BYTES = 56 * 2**20


def _params(n_axes):
    return pltpu.CompilerParams(dimension_semantics=("arbitrary",) * n_axes,
                                vmem_limit_bytes=VMEM_LIMIT_BYTES)


def _sigmoid(x):
    return 1.0 / (1.0 + jnp.exp(-x))


def _silu(x):
    return x * _sigmoid(x)


def _rms_to_bf16(x, g):
    ms = jnp.mean(x * x, axis=-1, keepdims=True)
    return ((x * lax.rsqrt(ms + EPS)) * g).astype(BF16)


def _row_tile(m, want):
    t = min(m, want)
    assert m % t == 0, (m, t)
    return t


def _linear_body(*refs, has_norm, n_w, n_extra, epilogue):
    it = iter(refs)
    x_ref = next(it)
    g_ref = next(it) if has_norm else None
    w_refs = [next(it) for _ in range(n_w)]
    extra_refs = [next(it) for _ in range(n_extra)]
    o_ref = next(it)
    if has_norm:
        xn_ref = next(it)

        @pl.when(pl.program_id(1) == 0)
        def _():
            xn_ref[...] = _rms_to_bf16(x_ref[...], g_ref[...])

        lhs = xn_ref[...]
    else:
        lhs = x_ref[...].astype(BF16)
    accs = [jnp.dot(lhs, w[...], preferred_element_type=F32) for w in w_refs]
    o_ref[...] = epilogue(accs, extra_refs, x_ref).astype(o_ref.dtype)


def _linear(x, w, *, n_out, tm, tn, epilogue, out_dtype, name, gain=None,
            w_offsets=(0,), extras=()):
    m, k = x.shape
    tm = _row_tile(m, tm)
    assert n_out % tn == 0
    in_specs = [pl.BlockSpec((tm, k), lambda i, j: (i, 0))]
    args = [x]
    if gain is not None:
        in_specs.append(pl.BlockSpec((1, k), lambda i, j: (0, 0)))
        args.append(gain.reshape(1, k))
    for off in w_offsets:
        in_specs.append(pl.BlockSpec((k, tn), lambda i, j, off=off: (0, j + off)))
        args.append(w)
    for arr, shape, imap in extras:
        in_specs.append(pl.BlockSpec(shape, imap))
        args.append(arr)
    body = functools.partial(_linear_body, has_norm=gain is not None, n_w=len(w_offsets),
                             n_extra=len(extras), epilogue=epilogue)
    return pl.pallas_call(
        body,
        grid=(m // tm, n_out // tn),
        in_specs=in_specs,
        out_specs=pl.BlockSpec((tm, tn), lambda i, j: (i, j)),
        out_shape=jax.ShapeDtypeStruct((m, n_out), out_dtype),
        scratch_shapes=[pltpu.VMEM((tm, k), BF16)] if gain is not None else [],
        compiler_params=_params(2),
        name=name,
    )(*args)


def _ep_plain(accs, extras, x_ref):
    return accs[0]


def _ep_residual(accs, extras, x_ref):
    return extras[0][...] + accs[0]


def _ep_glu(accs, extras, x_ref):
    return accs[0] * _sigmoid(accs[1])


def _ep_ple(accs, extras, x_ref):
    p_ref, wp_ref = extras
    proj = jnp.dot(p_ref[...].astype(BF16), wp_ref[...], preferred_element_type=F32)
    return x_ref[...] + proj * _sigmoid(accs[0])


def _residual_extra(h, tm, tn):
    return (h, (_row_tile(h.shape[0], tm), tn), lambda i, j: (i, j))


def _rope_tables(pos):
    half = A_HEAD_DIM // 2
    inv = 1.0 / (ROPE_THETA ** (jnp.arange(half, dtype=F32) / half))
    ang = pos.astype(F32)[:, None] * inv[None, :]
    cos, sin = jnp.cos(ang), jnp.sin(ang)
    return jnp.tile(cos, (1, 4)), jnp.tile(jnp.concatenate([-sin, sin], axis=1), (1, 2))


def _attn_prep_body(qkv_ref, cos_ref, sin_ref, qg_ref, kg_ref, q_ref, k_ref, *, n_q_pairs,
                    n_k_pairs):
    cos = cos_ref[...]
    sin = sin_ref[...]
    lane = lax.broadcasted_iota(jnp.int32, cos.shape, 1)
    left = lane < A_HEAD_DIM
    first_half = (lane & (A_HEAD_DIM // 2)) == 0
    for p in range(n_q_pairs + n_k_pairs):
        is_q = p < n_q_pairs
        x = qkv_ref[:, p * LANES:(p + 1) * LANES]
        sq = x * x
        s_left = jnp.sum(jnp.where(left, sq, 0.0), axis=-1, keepdims=True)
        s_right = jnp.sum(jnp.where(left, 0.0, sq), axis=-1, keepdims=True)
        ms = jnp.where(left, s_left, s_right) * (1.0 / A_HEAD_DIM)
        y = (x * lax.rsqrt(ms + EPS)) * (qg_ref[...] if is_q else kg_ref[...])
        partner = jnp.where(first_half, pltpu.roll(y, LANES - A_HEAD_DIM // 2, 1),
                            pltpu.roll(y, A_HEAD_DIM // 2, 1))
        out = y * cos + partner * sin
        if is_q:
            q_ref[p] = (out * (A_HEAD_DIM ** -0.5)).astype(BF16)
        else:
            pk = p - n_q_pairs
            k_ref[:, pk * LANES:(pk + 1) * LANES] = out


def _attn_prep(qkv, cos, sin, q_norm, k_norm, *, seq_len, n_heads):
    m, width = qkv.shape
    tm = _row_tile(m, 256)
    if seq_len < tm:
        cos = jnp.tile(cos, (tm // seq_len, 1))
        sin = jnp.tile(sin, (tm // seq_len, 1))
    n_tab = cos.shape[0] // tm
    n_q_pairs = n_heads * A_HEAD_DIM // LANES
    dk = A_KV_HEADS * A_HEAD_DIM
    body = functools.partial(_attn_prep_body, n_q_pairs=n_q_pairs, n_k_pairs=dk // LANES)
    tab_spec = pl.BlockSpec((tm, LANES), lambda i: (i % n_tab, 0))
    gain_spec = pl.BlockSpec((1, LANES), lambda i: (0, 0))
    return pl.pallas_call(
        body,
        grid=(m // tm,),
        in_specs=[pl.BlockSpec((tm, width), lambda i: (i, 0)), tab_spec, tab_spec, gain_spec,
                  gain_spec],
        out_specs=[pl.BlockSpec((n_q_pairs, tm, LANES), lambda i: (0, i, 0)),
                   pl.BlockSpec((tm, dk), lambda i: (i, 0))],
        out_shape=[jax.ShapeDtypeStruct((n_q_pairs, m, LANES), BF16),
                   jax.ShapeDtypeStruct((m, dk), F32)],
        compiler_params=_params(1),
        name="attn_prep",
    )(qkv, cos, sin, jnp.tile(q_norm, 2).reshape(1, LANES), jnp.tile(k_norm, 2).reshape(1, LANES))


def _even_odd_stack(x_all, g):
    pair = x_all[:, (g // 2) * LANES:(g // 2 + 1) * LANES]
    swapped = pltpu.roll(pair, A_HEAD_DIM, 1)
    low = lax.broadcasted_iota(jnp.int32, pair.shape, 1) < A_HEAD_DIM
    if g % 2 == 0:
        lo, hi = jnp.where(low, pair, 0.0), jnp.where(low, 0.0, swapped)
    else:
        lo, hi = jnp.where(low, swapped, 0.0), jnp.where(low, 0.0, pair)
    return jnp.concatenate([lo, hi], axis=0).astype(BF16)


def _attn_groups(q_ref, k_all, v_all, sinks_ref, o_ref, mask, n_heads):
    tq = q_ref.shape[1]
    s_len = k_all.shape[0]
    ppg = n_heads // A_KV_HEADS // 2
    row_pair = lax.broadcasted_iota(jnp.int32, (ppg * tq, 1), 0) // tq
    for g in range(A_KV_HEADS):
        k_cat = _even_odd_stack(k_all, g)
        v_cat = _even_odd_stack(v_all, g)
        q = q_ref[g * ppg:(g + 1) * ppg].reshape(ppg * tq, LANES)
        s = lax.dot_general(q, k_cat, (((1,), (1,)), ((), ())), preferred_element_type=F32)
        s = jnp.where(mask, s, -jnp.inf)
        probs = []
        for half in range(2):
            sh = s[:, half * s_len:(half + 1) * s_len]
            sink = jnp.zeros((ppg * tq, 1), F32)
            for p in range(ppg):
                sink = jnp.where(row_pair == p, sinks_ref[2 * (g * ppg + p) + half], sink)
            mx = jnp.maximum(jnp.max(sh, axis=-1, keepdims=True), sink)
            e = jnp.exp(sh - mx)
            denom = jnp.sum(e, axis=-1, keepdims=True) + jnp.exp(sink - mx)
            probs.append((e * (1.0 / denom)).astype(BF16))
        o = jnp.dot(jnp.concatenate(probs, axis=1), v_cat, preferred_element_type=F32)
        for p in range(ppg):
            col = (g * ppg + p) * LANES
            o_ref[:, col:col + LANES] = o[p * tq:(p + 1) * tq].astype(o_ref.dtype)


def _attn_prompt_body(sinks_ref, q_ref, kp_ref, kc_ref, vp_ref, vc_ref, o_ref, *, n_heads):
    tq = q_ref.shape[1]
    i = pl.program_id(1)
    k_all = jnp.concatenate([kp_ref[...], kc_ref[...]], axis=0)
    v_all = jnp.concatenate([vp_ref[...], vc_ref[...]], axis=0)
    s_len = WINDOW + tq
    rows = n_heads // A_KV_HEADS // 2 * tq
    assert tq & (tq - 1) == 0 and s_len & (s_len - 1) == 0 and CHUNK & (CHUNK - 1) == 0
    shift = CHUNK.bit_length() - 1
    q_chunk = (lax.broadcasted_iota(jnp.int32, (rows, 2 * s_len), 0) & (tq - 1)) >> shift
    k_chunk = ((lax.broadcasted_iota(jnp.int32, (rows, 2 * s_len), 1) & (s_len - 1)) >> shift) - WINDOW // CHUNK
    mask = (k_chunk <= q_chunk) & (k_chunk >= q_chunk - WINDOW // CHUNK)
    mask = mask & ((k_chunk >= 0) | (i > 0))
    _attn_groups(q_ref, k_all, v_all, sinks_ref, o_ref, mask, n_heads)


def _attn_prompt(q, k, qkv, sinks, *, batch, seq_len, n_heads):
    n_pairs, m, _ = q.shape
    tq = WINDOW
    nq = seq_len // tq
    dk = A_KV_HEADS * A_HEAD_DIM
    v_col = (qkv.shape[1] - dk) // dk
    d_out = n_heads * A_HEAD_DIM

    def prev_row(b, i):
        return jnp.maximum(b * nq + i - 1, 0)

    return pl.pallas_call(
        functools.partial(_attn_prompt_body, n_heads=n_heads),
        grid=(batch, nq),
        in_specs=[
            pl.BlockSpec(memory_space=pltpu.SMEM),
            pl.BlockSpec((n_pairs, tq, LANES), lambda b, i: (0, b * nq + i, 0)),
            pl.BlockSpec((WINDOW, dk), lambda b, i: (prev_row(b, i), 0)),
            pl.BlockSpec((tq, dk), lambda b, i: (b * nq + i, 0)),
            pl.BlockSpec((WINDOW, dk), lambda b, i: (prev_row(b, i), v_col)),
            pl.BlockSpec((tq, dk), lambda b, i: (b * nq + i, v_col)),
        ],
        out_specs=pl.BlockSpec((tq, d_out), lambda b, i: (b * nq + i, 0)),
        out_shape=jax.ShapeDtypeStruct((m, d_out), BF16),
        compiler_params=_params(2),
        name="attn_prompt",
    )(sinks, q, k, k, qkv, qkv)


def _attn_sample_body(sinks_ref, q_ref, kn_ref, vn_ref, ck_ref, cv_ref, o_ref, nk_ref, nv_ref, *,
                      n_heads):
    t = q_ref.shape[1]
    dk = kn_ref.shape[1]
    k_new, v_new = kn_ref[...], vn_ref[...]
    nk_ref[0, 0:WINDOW - t, :] = ck_ref[0, t:WINDOW, :]
    nk_ref[0, WINDOW - t:WINDOW, :] = k_new
    nv_ref[0, 0:WINDOW - t, :] = cv_ref[0, t:WINDOW, :]
    nv_ref[0, WINDOW - t:WINDOW, :] = v_new
    s_len = -(-(WINDOW + t) // LANES) * LANES
    zeros = jnp.zeros((s_len - WINDOW - t, dk), F32)
    k_all = jnp.concatenate([ck_ref[0], k_new, zeros], axis=0)
    v_all = jnp.concatenate([cv_ref[0], v_new, zeros], axis=0)
    rows = n_heads // A_KV_HEADS // 2 * t
    assert s_len & (s_len - 1) == 0
    mask = (lax.broadcasted_iota(jnp.int32, (rows, 2 * s_len), 1) & (s_len - 1)) < WINDOW + t
    _attn_groups(q_ref, k_all, v_all, sinks_ref, o_ref, mask, n_heads)


def _attn_sample(q, k, qkv, cache_k, cache_v, sinks, *, batch, seq_len, n_heads):
    n_pairs, m, _ = q.shape
    dk = A_KV_HEADS * A_HEAD_DIM
    v_col = (qkv.shape[1] - dk) // dk
    d_out = n_heads * A_HEAD_DIM
    cache_spec = pl.BlockSpec((1, WINDOW, dk), lambda b: (b, 0, 0))
    return pl.pallas_call(
        functools.partial(_attn_sample_body, n_heads=n_heads),
        grid=(batch,),
        in_specs=[
            pl.BlockSpec(memory_space=pltpu.SMEM),
            pl.BlockSpec((n_pairs, seq_len, LANES), lambda b: (0, b, 0)),
            pl.BlockSpec((seq_len, dk), lambda b: (b, 0)),
            pl.BlockSpec((seq_len, dk), lambda b: (b, v_col)),
            cache_spec, cache_spec,
        ],
        out_specs=[pl.BlockSpec((seq_len, d_out), lambda b: (b, 0)), cache_spec, cache_spec],
        out_shape=[jax.ShapeDtypeStruct((m, d_out), BF16),
                   jax.ShapeDtypeStruct((batch, WINDOW, dk), F32),
                   jax.ShapeDtypeStruct((batch, WINDOW, dk), F32)],
        compiler_params=_params(1),
        name="attn_sample",
    )(sinks, q, k, qkv, cache_k.reshape(batch, WINDOW, dk), cache_v.reshape(batch, WINDOW, dk))


CONV_HIST = 32
CONV_ROWS = 64
CONV_COLS = 256


def _conv_ln_body(u_ref, st_ref, w_ref, b_ref, g_ref, beta_ref, o_ref, ext_ref, c_ref, *, width):
    tm, d = u_ref.shape
    pad = CONV_HIST - (width - 1)

    @pl.when(pl.program_id(1) == 0)
    def _():
        ext_ref[0:CONV_HIST, :] = st_ref[0]

    ext_ref[CONV_HIST:CONV_HIST + tm, :] = u_ref[...]
    ext_ref[CONV_HIST + tm:CONV_HIST + tm + SUBLANES, :] = jnp.zeros((SUBLANES, d), F32)
    rows = min(CONV_ROWS, tm)
    for cs in range(0, d, CONV_COLS):
        for rb in range(0, tm, rows):
            acc = jnp.broadcast_to(b_ref[:, cs:cs + CONV_COLS], (rows, CONV_COLS))
            for r in range(SUBLANES):
                z = None
                for e in range(r, pad + width, SUBLANES):
                    if e < pad:
                        continue
                    blk = ext_ref[rb + e - r:rb + e - r + rows + SUBLANES, cs:cs + CONV_COLS]
                    term = blk * w_ref[e - pad:e - pad + 1, cs:cs + CONV_COLS]
                    z = term if z is None else z + term
                acc = acc + z[r:r + rows]
            c_ref[rb:rb + rows, cs:cs + CONV_COLS] = acc
    carry = ext_ref[tm:tm + CONV_HIST, :]
    ext_ref[0:CONV_HIST, :] = carry
    for rb in range(0, tm, rows):
        c = c_ref[rb:rb + rows, :]
        mu = jnp.mean(c, axis=-1, keepdims=True)
        xc = c - mu
        var = jnp.mean(xc * xc, axis=-1, keepdims=True)
        y = (xc * lax.rsqrt(var + EPS)) * g_ref[...] + beta_ref[...]
        o_ref[rb:rb + rows, :] = _silu(y).astype(o_ref.dtype)


def _conv_ln(u, state, w_dw, b_dw, ln_g, ln_b, *, batch, seq_len):
    m, d = u.shape
    width = w_dw.shape[0]
    assert width - 1 <= CONV_HIST
    tm = _row_tile(seq_len, 128)
    nt = seq_len // tm
    st = jnp.pad(state, ((0, 0), (CONV_HIST - (width - 1), 0), (0, 0)))
    w = jnp.pad(w_dw, ((0, CONV_HIST - width), (0, 0)))
    row = lambda a: a.reshape(1, d)
    vec_spec = pl.BlockSpec((1, d), lambda b, t: (0, 0))
    return pl.pallas_call(
        functools.partial(_conv_ln_body, width=width),
        grid=(batch, nt),
        in_specs=[pl.BlockSpec((tm, d), lambda b, t: (b * nt + t, 0)),
                  pl.BlockSpec((1, CONV_HIST, d), lambda b, t: (b, 0, 0)),
                  pl.BlockSpec((CONV_HIST, d), lambda b, t: (0, 0)),
                  vec_spec, vec_spec, vec_spec],
        out_specs=pl.BlockSpec((tm, d), lambda b, t: (b * nt + t, 0)),
        out_shape=jax.ShapeDtypeStruct((m, d), BF16),
        scratch_shapes=[pltpu.VMEM((CONV_HIST + tm + SUBLANES, d), F32), pltpu.VMEM((tm, d), F32)],
        compiler_params=_params(2),
        name="conv_ln",
    )(u, st, w, row(b_dw), row(ln_g), row(ln_b))


def _split3(x):
    h1 = x.astype(BF16)
    r1 = x - h1.astype(F32)
    h2 = r1.astype(BF16)
    h3 = (r1 - h2.astype(F32)).astype(BF16)
    return h1, h2, h3


def _exact_dot(dims, a_f32, b_bf16):
    return sum(lax.dot_general(piece, b_bf16, dims, preferred_element_type=F32)
               for piece in _split3(a_f32))


def _gla_body(proj_ref, wgu_ref, bias_ref, norm_ref, s0_ref, o_ref, sout_ref, s_ref, *, dk, dv):
    c = proj_ref.shape[0]
    dkh, dvh = dk // GLA_HEADS, dv // GLA_HEADS
    step = pl.program_id(1)

    @pl.when(step == 0)
    def _():
        s_ref[...] = s0_ref[0]

    gl = proj_ref[:, 2 * dk + 2 * dv:2 * dk + 2 * dv + LANES].astype(BF16)
    logit = jnp.dot(gl, wgu_ref[...], preferred_element_type=F32) + bias_ref[...]
    log_a = (jnp.minimum(logit, 0.0) - jnp.log1p(jnp.exp(-jnp.abs(logit)))) * (1.0 / GLA_TAU)
    ri = lax.broadcasted_iota(jnp.int32, (c, c), 0)
    ci = lax.broadcasted_iota(jnp.int32, (c, c), 1)
    causal = ri >= ci
    tri = jnp.where(causal, 1.0, 0.0).astype(BF16)
    b_all = sum(jnp.dot(tri, piece, preferred_element_type=F32) for piece in _split3(log_a))
    last_sel = jnp.where(lax.broadcasted_iota(jnp.int32, (c, LANES), 0) == c - 1, 1.0, 0.0).astype(BF16)
    tn_dims = (((0,), (0,)), ((), ()))
    for h in range(GLA_HEADS):
        b = b_all[:, h * dkh:(h + 1) * dkh]
        b_last = b[c - 1:c, :]
        q = proj_ref[:, h * dkh:(h + 1) * dkh] * (dkh ** -0.5)
        k = proj_ref[:, dk + h * dkh:dk + (h + 1) * dkh]
        v = proj_ref[:, 2 * dk + h * dvh:2 * dk + (h + 1) * dvh].astype(BF16)
        r = proj_ref[:, 2 * dk + dv + h * dvh:2 * dk + dv + (h + 1) * dvh]
        qt = (q * jnp.exp(b)).astype(BF16)
        kt = (k * jnp.exp(-b)).astype(BF16)
        k_dec = (k * jnp.exp(b_last - b)).astype(BF16)
        s_old = s_ref[h]
        o_inter = jnp.dot(qt, s_old.astype(BF16), preferred_element_type=F32)
        a = lax.dot_general(qt, kt, (((1,), (1,)), ((), ())), preferred_element_type=F32)
        a = jnp.where(causal, a, 0.0).astype(BF16)
        o = o_inter + jnp.dot(a, v, preferred_element_type=F32)
        b_last_col = _exact_dot(tn_dims, b, last_sel)
        decay = jnp.exp(b_last_col[:, 0:1])
        s_ref[h] = decay * s_old + lax.dot_general(k_dec, v, tn_dims, preferred_element_type=F32)
        ms = jnp.mean(o * o, axis=-1, keepdims=True)
        o_n = (o * lax.rsqrt(ms + EPS)) * norm_ref[...]
        o_ref[:, h * dvh:(h + 1) * dvh] = (o_n * _silu(r)).astype(o_ref.dtype)

    @pl.when(step == pl.num_programs(1) - 1)
    def _():
        sout_ref[0] = s_ref[...]


def _gla(proj, wgu, bias, out_norm, state, *, batch, seq_len, dk, dv):
    m, width = proj.shape
    c = min(CHUNK, seq_len)
    nc = seq_len // c
    dkh, dvh = dk // GLA_HEADS, dv // GLA_HEADS
    rank = wgu.shape[0]
    assert rank <= LANES and 2 * dk + 2 * dv + LANES <= width
    wgu_pad = jnp.pad(wgu, ((0, LANES - rank), (0, 0))).astype(BF16)
    state_spec = pl.BlockSpec((1, GLA_HEADS, dkh, dvh), lambda b, t: (b, 0, 0, 0))
    return pl.pallas_call(
        functools.partial(_gla_body, dk=dk, dv=dv),
        grid=(batch, nc),
        in_specs=[pl.BlockSpec((c, width), lambda b, t: (b * nc + t, 0)),
                  pl.BlockSpec(wgu_pad.shape, lambda b, t: (0, 0)),
                  pl.BlockSpec((1, dk), lambda b, t: (0, 0)),
                  pl.BlockSpec((1, dvh), lambda b, t: (0, 0)),
                  state_spec],
        out_specs=[pl.BlockSpec((c, dv), lambda b, t: (b * nc + t, 0)), state_spec],
        out_shape=[jax.ShapeDtypeStruct((m, dv), BF16),
                   jax.ShapeDtypeStruct((batch, GLA_HEADS, dkh, dvh), F32)],
        scratch_shapes=[pltpu.VMEM((GLA_HEADS, dkh, dvh), F32)],
        compiler_params=_params(2),
        name="gla",
    )(proj, wgu_pad, bias.reshape(1, dk), out_norm.reshape(1, dvh), state)


def _ffn_body(x_ref, gain_ref, wg_ref, wu_ref, cw_ref, cb_ref, wd_ref, st_ref, o_ref, ns_ref,
              xn_ref, *carry, seq_tile, tiles_per_seq):
    tm = x_ref.shape[0]
    tn = wg_ref.shape[1]
    nseq = tm // seq_tile
    i = pl.program_id(0)
    j = pl.program_id(1)

    @pl.when(j == 0)
    def _():
        x = x_ref[...]
        xn_ref[...] = _rms_to_bf16(x, gain_ref[...])
        o_ref[...] = x

    xn = xn_ref[...]
    g = jnp.dot(xn, wg_ref[...], preferred_element_type=F32)
    u = jnp.dot(xn, wu_ref[...], preferred_element_type=F32)
    if tiles_per_seq > 1:
        prev = jnp.where(i % tiles_per_seq == 0, st_ref[...], carry[0][j])
    else:
        prev = st_ref[...]
    g3 = g.reshape(nseq, seq_tile, tn)
    row = lax.broadcasted_iota(jnp.int32, (nseq, seq_tile, tn), 1)
    back1 = jnp.where(row == 0, prev[:, 1:2, :], pltpu.roll(g, 1, 0).reshape(nseq, seq_tile, tn))
    back2 = jnp.where(row == 0, prev[:, 0:1, :],
                      jnp.where(row == 1, prev[:, 1:2, :],
                                pltpu.roll(g, 2, 0).reshape(nseq, seq_tile, tn)))
    gc = (back2 * cw_ref[0:1, :] + back1 * cw_ref[1:2, :] + g3 * cw_ref[2:3, :]) + cb_ref[...]
    act = (_silu(gc).reshape(tm, tn) * u).astype(BF16)
    o_ref[...] += jnp.dot(act, wd_ref[...], preferred_element_type=F32)
    tail = g3[:, seq_tile - 2:seq_tile, :]
    ns_ref[...] = tail
    if tiles_per_seq > 1:
        carry[0][j] = tail


def _ffn(h, gain, w_up, conv_w, conv_b, w_down, state, *, seq_len, tn=512):
    m, d = h.shape
    dff = w_down.shape[0]
    assert dff % tn == 0
    nj = dff // tn
    tm = _row_tile(m, 1024)
    seq_tile = min(seq_len, tm)
    tiles_per_seq = seq_len // seq_tile
    nseq = tm // seq_tile
    width = conv_w.shape[0]
    assert width == 3 and state.shape[1] == width - 1
    st_map = lambda i, j: (i // tiles_per_seq, 0, j)
    tail_map = lambda i, j: (i, 0, j)
    scratch = [pltpu.VMEM((tm, d), BF16)]
    if tiles_per_seq > 1:
        scratch.append(pltpu.VMEM((nj, nseq, width - 1, tn), F32))
    out, tails = pl.pallas_call(
        functools.partial(_ffn_body, seq_tile=seq_tile, tiles_per_seq=tiles_per_seq),
        grid=(m // tm, nj),
        in_specs=[pl.BlockSpec((tm, d), lambda i, j: (i, 0)),
                  pl.BlockSpec((1, d), lambda i, j: (0, 0)),
                  pl.BlockSpec((d, tn), lambda i, j: (0, j)),
                  pl.BlockSpec((d, tn), lambda i, j: (0, j + nj)),
                  pl.BlockSpec((width, tn), lambda i, j: (0, j)),
                  pl.BlockSpec((1, tn), lambda i, j: (0, j)),
                  pl.BlockSpec((tn, d), lambda i, j: (j, 0)),
                  pl.BlockSpec((nseq, width - 1, tn), st_map)],
        out_specs=[pl.BlockSpec((tm, d), lambda i, j: (i, 0)),
                   pl.BlockSpec((nseq, width - 1, tn), tail_map)],
        out_shape=[jax.ShapeDtypeStruct((m, d), F32),
                   jax.ShapeDtypeStruct((m // seq_tile, width - 1, dff), F32)],
        scratch_shapes=scratch,
        compiler_params=_params(2),
        name="conv_ffn",
    )(h, gain.reshape(1, d), w_up, w_up, conv_w, conv_b.reshape(1, dff), w_down, state)
    return out, tails.reshape(-1, tiles_per_seq, width - 1, dff)[:, -1]


def _run_group(h, p, *, batch, seq_len, first_pos, states, weights):
    (cache_k, cache_v, st_conv, st_gla, st_ffn) = states
    w = weights
    m, d = h.shape
    depth = w["norm_mix"].shape[0]
    n_heads = w["a_w_o"][0].shape[0] // A_HEAD_DIM
    dkv = A_KV_HEADS * A_HEAD_DIM
    cos, sin = _rope_tables(first_pos + jnp.arange(seq_len))
    new_k, new_v, new_conv, new_gla, new_ffn = [], [], [], [], []
    for layer in range(depth):
        kind, slot = layer % 3, layer // 3
        if kind == 0:
            wqkv = w["a_w_qkv"][slot]
            n_qkv = wqkv.shape[1]
            qkv = _linear(h, wqkv, n_out=n_qkv, tm=1024, tn=n_qkv // 2, epilogue=_ep_plain,
                          out_dtype=F32, gain=w["norm_mix"][layer], name="attn_qkv")
            q, k = _attn_prep(qkv, cos, sin, w["a_q_norm"][slot], w["a_k_norm"][slot],
                              seq_len=seq_len, n_heads=n_heads)
            if cache_k is None:
                o = _attn_prompt(q, k, qkv, w["a_sinks"][slot], batch=batch, seq_len=seq_len,
                                 n_heads=n_heads)
                new_k.append(k.reshape(batch, seq_len, dkv)[:, seq_len - WINDOW:])
                new_v.append(qkv.reshape(batch, seq_len, n_qkv)[:, seq_len - WINDOW:, n_qkv - dkv:])
            else:
                o, nk, nv = _attn_sample(q, k, qkv, cache_k[slot], cache_v[slot], w["a_sinks"][slot],
                                         batch=batch, seq_len=seq_len, n_heads=n_heads)
                new_k.append(nk)
                new_v.append(nv)
            mix_in, mix_w = o, w["a_w_o"][slot]
        elif kind == 1:
            tn = 1024
            u = _linear(h, w["b_w_pw1"][slot], n_out=d, tm=1024, tn=tn, epilogue=_ep_glu,
                        out_dtype=F32, gain=w["norm_mix"][layer], w_offsets=(0, d // tn),
                        name="conf_pw1_glu")
            hist = w["b_w_dw"].shape[1] - 1
            if st_conv is None:
                conv_state = jnp.zeros((batch, hist, d), F32)
            else:
                conv_state = st_conv[slot]
            mix_in = _conv_ln(u, conv_state, w["b_w_dw"][slot], w["b_dw_bias"][slot],
                              w["b_ln_g"][slot], w["b_ln_b"][slot], batch=batch, seq_len=seq_len)
            u3 = u.reshape(batch, seq_len, d)
            if seq_len >= hist:
                new_conv.append(u3[:, seq_len - hist:])
            else:
                new_conv.append(jnp.concatenate([conv_state, u3], axis=1)[:, seq_len:])
            mix_w = w["b_w_pw2"][slot]
        else:
            w_in = w["c_w_in"][slot]
            dk = w["c_w_gate_up"].shape[2]
            dv = w["c_w_o"][slot].shape[0]
            proj = _linear(h, w_in, n_out=w_in.shape[1], tm=1024, tn=GLA_IN_TN,
                           epilogue=_ep_plain, out_dtype=F32, gain=w["norm_mix"][layer],
                           name="gla_in")
            if st_gla is None:
                s0 = jnp.zeros((batch, GLA_HEADS, dk // GLA_HEADS, dv // GLA_HEADS), F32)
            else:
                s0 = st_gla[slot]
            mix_in, s_new = _gla(proj, w["c_w_gate_up"][slot], w["c_gate_bias"][slot],
                                 w["c_out_norm"][slot], s0, batch=batch, seq_len=seq_len, dk=dk, dv=dv)
            new_gla.append(s_new)
            mix_w = w["c_w_o"][slot]
        h = _linear(mix_in, mix_w, n_out=d, tm=512, tn=d, epilogue=_ep_residual, out_dtype=F32,
                    extras=[_residual_extra(h, 512, d)], name="mixer_out")
        dff = w["ffn_w_down"][layer].shape[0]
        if st_ffn is None:
            ffn_state = jnp.zeros((batch, w["ffn_conv_w"].shape[1] - 1, dff), F32)
        else:
            ffn_state = st_ffn[layer]
        h, ns = _ffn(h, w["norm_ffn"][layer], w["ffn_w_up"][layer], w["ffn_conv_w"][layer],
                     w["ffn_conv_b"][layer], w["ffn_w_down"][layer], ffn_state, seq_len=seq_len)
        new_ffn.append(ns)
        tm = _row_tile(m, 512)
        ple_dim = p.shape[-1]
        h = _linear(h, w["ple_w_gate"][layer], n_out=d, tm=512, tn=d, epilogue=_ep_ple,
                    out_dtype=F32, gain=w["ple_norm"][layer],
                    extras=[(p, (None, tm, ple_dim), lambda i, j, layer=layer: (layer, i, 0)),
                            (w["ple_w_proj"][layer], (ple_dim, d), lambda i, j: (0, j))],
                    name="ple")
    kv_shape = (len(new_k), batch, WINDOW, A_KV_HEADS, A_HEAD_DIM)
    return (h.reshape(batch, seq_len, d),
            jnp.stack(new_k).reshape(kv_shape), jnp.stack(new_v).reshape(kv_shape),
            jnp.stack(new_conv), jnp.stack(new_gla), jnp.stack(new_ffn))


def _bf16_layers(w):
    return [w[i].astype(BF16) for i in range(w.shape[0])]


def kernel(x_prompt, x_sample, p_prompt, p_sample, cache_k_a, cache_v_a, state_conv_b, state_gla_c, state_ffn_conv, norm_mix, norm_ffn, a_w_qkv, a_q_norm, a_k_norm, a_sinks, a_w_o, b_w_pw1, b_w_dw, b_dw_bias, b_ln_g, b_ln_b, b_w_pw2, c_w_in, c_w_gate_up, c_gate_bias, c_out_norm, c_w_o, ffn_w_up, ffn_conv_w, ffn_conv_b, ffn_w_down, ple_w_proj, ple_norm, ple_w_gate):
    bp, lp, d = x_prompt.shape
    bs, ls, _ = x_sample.shape
    depth = norm_mix.shape[0]
    gla_cols = c_w_in.shape[2]
    gla_pad = (-gla_cols) % GLA_IN_TN
    weights = dict(
        norm_mix=norm_mix, norm_ffn=norm_ffn,
        a_w_qkv=_bf16_layers(a_w_qkv), a_q_norm=a_q_norm, a_k_norm=a_k_norm, a_sinks=a_sinks,
        a_w_o=_bf16_layers(a_w_o),
        b_w_pw1=_bf16_layers(b_w_pw1), b_w_dw=b_w_dw, b_dw_bias=b_dw_bias, b_ln_g=b_ln_g,
        b_ln_b=b_ln_b, b_w_pw2=_bf16_layers(b_w_pw2),
        c_w_in=_bf16_layers(jnp.pad(c_w_in, ((0, 0), (0, 0), (0, gla_pad)))),
        c_w_gate_up=c_w_gate_up, c_gate_bias=c_gate_bias, c_out_norm=c_out_norm,
        c_w_o=_bf16_layers(c_w_o),
        ffn_w_up=_bf16_layers(ffn_w_up), ffn_conv_w=ffn_conv_w, ffn_conv_b=ffn_conv_b,
        ffn_w_down=_bf16_layers(ffn_w_down),
        ple_w_proj=_bf16_layers(ple_w_proj), ple_norm=ple_norm, ple_w_gate=_bf16_layers(ple_w_gate),
    )
    prompt = _run_group(x_prompt.reshape(bp * lp, d), p_prompt.reshape(depth, bp * lp, -1),
                        batch=bp, seq_len=lp, first_pos=0,
                        states=(None, None, None, None, None), weights=weights)
    sample = _run_group(x_sample.reshape(bs * ls, d), p_sample.reshape(depth, bs * ls, -1),
                        batch=bs, seq_len=ls, first_pos=PAST_LEN,
                        states=(cache_k_a, cache_v_a, state_conv_b, state_gla_c, state_ffn_conv),
                        weights=weights)
    return (prompt[0], sample[0]) + prompt[1:] + sample[1:]
```

```python
import functools

import jax
import jax.numpy as jnp
from jax import lax
from jax.experimental import pallas as pl
from jax.experimental.pallas import tpu as pltpu

F32 = jnp.float32
BF16 = jnp.bfloat16

EPS = 1e-6
CHUNK = 64
WINDOW = 128
A_HEAD_DIM = 64
A_KV_HEADS = 4
ROPE_THETA = 10000.0
PAST_LEN = 2048
GLA_HEADS = 4
GLA_TAU = 16.0
GLA_IN_TN = 1280
LANES = 128
SUBLANES = 8
VMEM_LIMIT_BYTES = 56 * 2**20


def _params(n_axes):
    return pltpu.CompilerParams(dimension_semantics=("arbitrary",) * n_axes,
                                vmem_limit_bytes=VMEM_LIMIT_BYTES)


def _sigmoid(x):
    return 1.0 / (1.0 + jnp.exp(-x))


def _silu(x):
    return x * _sigmoid(x)


def _rms_to_bf16(x, g):
    ms = jnp.mean(x * x, axis=-1, keepdims=True)
    return ((x * lax.rsqrt(ms + EPS)) * g).astype(BF16)


def _row_tile(m, want):
    t = min(m, want)
    assert m % t == 0, (m, t)
    return t


def _linear_body(*refs, has_norm, n_w, n_extra, epilogue):
    it = iter(refs)
    x_ref = next(it)
    g_ref = next(it) if has_norm else None
    w_refs = [next(it) for _ in range(n_w)]
    extra_refs = [next(it) for _ in range(n_extra)]
    o_ref = next(it)
    if has_norm:
        xn_ref = next(it)

        @pl.when(pl.program_id(1) == 0)
        def _():
            xn_ref[...] = _rms_to_bf16(x_ref[...], g_ref[...])

        lhs = xn_ref[...]
    else:
        lhs = x_ref[...].astype(BF16)
    accs = [jnp.dot(lhs, w[...], preferred_element_type=F32) for w in w_refs]
    o_ref[...] = epilogue(accs, extra_refs, x_ref).astype(o_ref.dtype)


def _linear(x, w, layer, *, n_out, tm, tn, epilogue, out_dtype, name, gain=None,
            w_offsets=(0,), extras=()):
    m, k = x.shape
    tm = _row_tile(m, tm)
    assert n_out % tn == 0
    in_specs = [pl.BlockSpec((tm, k), lambda i, j: (i, 0))]
    args = [x]
    if gain is not None:
        in_specs.append(pl.BlockSpec((1, k), lambda i, j: (0, 0)))
        args.append(gain.reshape(1, k))
    for off in w_offsets:
        in_specs.append(pl.BlockSpec((None, k, tn), lambda i, j, off=off: (layer, 0, j + off)))
        args.append(w)
    for arr, shape, imap in extras:
        in_specs.append(pl.BlockSpec(shape, imap))
        args.append(arr)
    body = functools.partial(_linear_body, has_norm=gain is not None, n_w=len(w_offsets),
                             n_extra=len(extras), epilogue=epilogue)
    return pl.pallas_call(
        body,
        grid=(m // tm, n_out // tn),
        in_specs=in_specs,
        out_specs=pl.BlockSpec((tm, tn), lambda i, j: (i, j)),
        out_shape=jax.ShapeDtypeStruct((m, n_out), out_dtype),
        scratch_shapes=[pltpu.VMEM((tm, k), BF16)] if gain is not None else [],
        compiler_params=_params(2),
        name=name,
    )(*args)


def _ep_plain(accs, extras, x_ref):
    return accs[0]


def _ep_residual(accs, extras, x_ref):
    return extras[0][...] + accs[0]


def _ep_glu(accs, extras, x_ref):
    return accs[0] * _sigmoid(accs[1])


def _ep_ple(accs, extras, x_ref):
    p_ref, wp_ref = extras
    proj = jnp.dot(p_ref[...].astype(BF16), wp_ref[...], preferred_element_type=F32)
    return x_ref[...] + proj * _sigmoid(accs[0])


def _residual_extra(h, tm, tn):
    return (h, (_row_tile(h.shape[0], tm), tn), lambda i, j: (i, j))


def _rope_tables(pos):
    half = A_HEAD_DIM // 2
    inv = 1.0 / (ROPE_THETA ** (jnp.arange(half, dtype=F32) / half))
    ang = pos.astype(F32)[:, None] * inv[None, :]
    cos, sin = jnp.cos(ang), jnp.sin(ang)
    return jnp.tile(cos, (1, 4)), jnp.tile(jnp.concatenate([-sin, sin], axis=1), (1, 2))


def _attn_in_body(x_ref, gain_ref, w_ref, cos_ref, sin_ref, qg_ref, kg_ref, q_ref, k_ref, v_ref, *,
                  n_q_pairs, n_k_pairs):
    xn = _rms_to_bf16(x_ref[...], gain_ref[...])
    cos = cos_ref[...]
    sin = sin_ref[...]
    lane = lax.broadcasted_iota(jnp.int32, cos.shape, 1)
    left = lane < A_HEAD_DIM
    first_half = (lane & (A_HEAD_DIM // 2)) == 0
    n_pairs = w_ref.shape[1] // LANES
    for c in range(0, n_pairs, 2):
        acc = jnp.dot(xn, w_ref[:, c * LANES:(c + 2) * LANES], preferred_element_type=F32)
        for p in (c, c + 1):
            x = acc[:, (p - c) * LANES:(p - c + 1) * LANES]
            if p >= n_q_pairs + n_k_pairs:
                pv = p - n_q_pairs - n_k_pairs
                v_ref[:, pv * LANES:(pv + 1) * LANES] = x
                continue
            is_q = p < n_q_pairs
            sq = x * x
            s_left = jnp.sum(jnp.where(left, sq, 0.0), axis=-1, keepdims=True)
            s_right = jnp.sum(jnp.where(left, 0.0, sq), axis=-1, keepdims=True)
            ms = jnp.where(left, s_left, s_right) * (1.0 / A_HEAD_DIM)
            y = (x * lax.rsqrt(ms + EPS)) * (qg_ref[...] if is_q else kg_ref[...])
            partner = jnp.where(first_half, pltpu.roll(y, LANES - A_HEAD_DIM // 2, 1),
                                pltpu.roll(y, A_HEAD_DIM // 2, 1))
            out = y * cos + partner * sin
            if is_q:
                q_ref[p] = (out * (A_HEAD_DIM ** -0.5)).astype(BF16)
            else:
                pk = p - n_q_pairs
                k_ref[:, pk * LANES:(pk + 1) * LANES] = out


def _attn_in(h, gain, w_qkv, slot, cos, sin, q_norm, k_norm, *, seq_len, n_heads):
    m, d = h.shape
    width = w_qkv.shape[2]
    tm = _row_tile(m, 512)
    if seq_len < tm:
        cos = jnp.tile(cos, (tm // seq_len, 1))
        sin = jnp.tile(sin, (tm // seq_len, 1))
    n_tab = cos.shape[0] // tm
    n_q_pairs = n_heads * A_HEAD_DIM // LANES
    dk = A_KV_HEADS * A_HEAD_DIM
    assert width == n_q_pairs * LANES + 2 * dk and (width // LANES) % 2 == 0
    body = functools.partial(_attn_in_body, n_q_pairs=n_q_pairs, n_k_pairs=dk // LANES)
    tab_spec = pl.BlockSpec((tm, LANES), lambda i: (i % n_tab, 0))
    gain_spec = pl.BlockSpec((1, LANES), lambda i: (0, 0))
    kv_spec = pl.BlockSpec((tm, dk), lambda i: (i, 0))
    return pl.pallas_call(
        body,
        grid=(m // tm,),
        in_specs=[pl.BlockSpec((tm, d), lambda i: (i, 0)),
                  pl.BlockSpec((1, d), lambda i: (0, 0)),
                  pl.BlockSpec((None, d, width), lambda i: (slot, 0, 0)),
                  tab_spec, tab_spec, gain_spec, gain_spec],
        out_specs=[pl.BlockSpec((n_q_pairs, tm, LANES), lambda i: (0, i, 0)), kv_spec, kv_spec],
        out_shape=[jax.ShapeDtypeStruct((n_q_pairs, m, LANES), BF16),
                   jax.ShapeDtypeStruct((m, dk), F32), jax.ShapeDtypeStruct((m, dk), F32)],
        compiler_params=_params(1),
        name="attn_in",
    )(h, gain.reshape(1, d), w_qkv, cos, sin, jnp.tile(q_norm, 2).reshape(1, LANES),
      jnp.tile(k_norm, 2).reshape(1, LANES))


def _even_odd_stack(x_all, g):
    pair = x_all[:, (g // 2) * LANES:(g // 2 + 1) * LANES]
    swapped = pltpu.roll(pair, A_HEAD_DIM, 1)
    low = lax.broadcasted_iota(jnp.int32, pair.shape, 1) < A_HEAD_DIM
    if g % 2 == 0:
        lo, hi = jnp.where(low, pair, 0.0), jnp.where(low, 0.0, swapped)
    else:
        lo, hi = jnp.where(low, swapped, 0.0), jnp.where(low, 0.0, pair)
    return jnp.concatenate([lo, hi], axis=0).astype(BF16)


def _attn_groups(q_ref, k_all, v_all, sinks_ref, bias_ref, o_ref, n_heads):
    tq = q_ref.shape[1]
    s_len = k_all.shape[0]
    ppg = n_heads // A_KV_HEADS // 2
    row_pair = lax.broadcasted_iota(jnp.int32, (ppg * tq, 1), 0) // tq
    for g in range(A_KV_HEADS):
        k_cat = _even_odd_stack(k_all, g)
        v_cat = _even_odd_stack(v_all, g)
        q = q_ref[g * ppg:(g + 1) * ppg].reshape(ppg * tq, LANES)
        s = lax.dot_general(q, k_cat, (((1,), (1,)), ((), ())), preferred_element_type=F32)
        s = s + bias_ref[...]
        probs = []
        for half in range(2):
            sh = s[:, half * s_len:(half + 1) * s_len]
            sink = jnp.zeros((ppg * tq, 1), F32)
            for p in range(ppg):
                sink = jnp.where(row_pair == p, sinks_ref[2 * (g * ppg + p) + half], sink)
            mx = jnp.maximum(jnp.max(sh, axis=-1, keepdims=True), sink)
            e = jnp.exp(sh - mx)
            denom = jnp.sum(e, axis=-1, keepdims=True) + jnp.exp(sink - mx)
            probs.append((e * (1.0 / denom)).astype(BF16))
        o = jnp.dot(jnp.concatenate(probs, axis=1), v_cat, preferred_element_type=F32)
        for p in range(ppg):
            col = (g * ppg + p) * LANES
            o_ref[:, col:col + LANES] = o[p * tq:(p + 1) * tq].astype(o_ref.dtype)


def _attn_prompt_body(sinks_ref, bias_ref, q_ref, kp_ref, kc_ref, vp_ref, vc_ref, o_ref, *, n_heads):
    k_all = jnp.concatenate([kp_ref[...], kc_ref[...]], axis=0)
    v_all = jnp.concatenate([vp_ref[...], vc_ref[...]], axis=0)
    _attn_groups(q_ref, k_all, v_all, sinks_ref, bias_ref, o_ref, n_heads)


def _band_bias(tq, rows):
    s_len = WINDOW + tq
    q_chunk = (jnp.arange(rows)[:, None] % tq) // CHUNK
    k_chunk = (jnp.arange(2 * s_len)[None, :] % s_len) // CHUNK - WINDOW // CHUNK
    band = (k_chunk <= q_chunk) & (k_chunk >= q_chunk - WINDOW // CHUNK)
    visible = jnp.stack([band & (k_chunk >= 0), band])
    return jnp.where(visible, 0.0, -jnp.inf).astype(F32)


def _attn_prompt(q, k, v, sinks, *, batch, seq_len, n_heads):
    n_pairs, m, _ = q.shape
    tq = WINDOW
    nq = seq_len // tq
    dk = A_KV_HEADS * A_HEAD_DIM
    d_out = n_heads * A_HEAD_DIM
    rows = n_pairs // A_KV_HEADS * tq
    bias = _band_bias(tq, rows)

    def prev_row(b, i):
        return jnp.maximum(b * nq + i - 1, 0)

    prev_spec = pl.BlockSpec((WINDOW, dk), lambda b, i: (prev_row(b, i), 0))
    cur_spec = pl.BlockSpec((tq, dk), lambda b, i: (b * nq + i, 0))
    return pl.pallas_call(
        functools.partial(_attn_prompt_body, n_heads=n_heads),
        grid=(batch, nq),
        in_specs=[
            pl.BlockSpec(memory_space=pltpu.SMEM),
            pl.BlockSpec((None,) + bias.shape[1:], lambda b, i: (jnp.minimum(i, 1), 0, 0)),
            pl.BlockSpec((n_pairs, tq, LANES), lambda b, i: (0, b * nq + i, 0)),
            prev_spec, cur_spec, prev_spec, cur_spec,
        ],
        out_specs=pl.BlockSpec((tq, d_out), lambda b, i: (b * nq + i, 0)),
        out_shape=jax.ShapeDtypeStruct((m, d_out), BF16),
        compiler_params=_params(2),
        name="attn_prompt",
    )(sinks, bias, q, k, k, v, v)


def _attn_sample_body(sinks_ref, bias_ref, q_ref, kn_ref, vn_ref, ck_ref, cv_ref, o_ref, nk_ref,
                      nv_ref, *, n_heads):
    t = q_ref.shape[1]
    dk = kn_ref.shape[1]
    k_new, v_new = kn_ref[...], vn_ref[...]
    nk_ref[0, 0:WINDOW - t, :] = ck_ref[0, t:WINDOW, :]
    nk_ref[0, WINDOW - t:WINDOW, :] = k_new
    nv_ref[0, 0:WINDOW - t, :] = cv_ref[0, t:WINDOW, :]
    nv_ref[0, WINDOW - t:WINDOW, :] = v_new
    s_len = bias_ref.shape[1] // 2
    zeros = jnp.zeros((s_len - WINDOW - t, dk), F32)
    k_all = jnp.concatenate([ck_ref[0], k_new, zeros], axis=0)
    v_all = jnp.concatenate([cv_ref[0], v_new, zeros], axis=0)
    _attn_groups(q_ref, k_all, v_all, sinks_ref, bias_ref, o_ref, n_heads)


def _attn_sample(q, k, v, cache_k, cache_v, sinks, *, batch, seq_len, n_heads):
    n_pairs, m, _ = q.shape
    dk = A_KV_HEADS * A_HEAD_DIM
    d_out = n_heads * A_HEAD_DIM
    rows = n_pairs // A_KV_HEADS * seq_len
    s_len = -(-(WINDOW + seq_len) // LANES) * LANES
    visible = (jnp.arange(2 * s_len)[None, :] % s_len) < WINDOW + seq_len
    bias = jnp.broadcast_to(jnp.where(visible, 0.0, -jnp.inf).astype(F32), (rows, 2 * s_len))
    cache_spec = pl.BlockSpec((1, WINDOW, dk), lambda b: (b, 0, 0))
    new_spec = pl.BlockSpec((seq_len, dk), lambda b: (b, 0))
    return pl.pallas_call(
        functools.partial(_attn_sample_body, n_heads=n_heads),
        grid=(batch,),
        in_specs=[
            pl.BlockSpec(memory_space=pltpu.SMEM),
            pl.BlockSpec(bias.shape, lambda b: (0, 0)),
            pl.BlockSpec((n_pairs, seq_len, LANES), lambda b: (0, b, 0)),
            new_spec, new_spec, cache_spec, cache_spec,
        ],
        out_specs=[pl.BlockSpec((seq_len, d_out), lambda b: (b, 0)), cache_spec, cache_spec],
        out_shape=[jax.ShapeDtypeStruct((m, d_out), BF16),
                   jax.ShapeDtypeStruct((batch, WINDOW, dk), F32),
                   jax.ShapeDtypeStruct((batch, WINDOW, dk), F32)],
        compiler_params=_params(1),
        name="attn_sample",
    )(sinks, bias, q, k, v, cache_k.reshape(batch, WINDOW, dk), cache_v.reshape(batch, WINDOW, dk))


CONV_HIST = 32
CONV_ROWS = 64
CONV_COLS = 256


def _conv_ln_body(u_ref, st_ref, w_ref, b_ref, g_ref, beta_ref, o_ref, ext_ref, c_ref, *, width):
    tm, d = u_ref.shape
    pad = CONV_HIST - (width - 1)

    @pl.when(pl.program_id(1) == 0)
    def _():
        ext_ref[0:CONV_HIST, :] = st_ref[0]

    ext_ref[CONV_HIST:CONV_HIST + tm, :] = u_ref[...]
    ext_ref[CONV_HIST + tm:CONV_HIST + tm + SUBLANES, :] = jnp.zeros((SUBLANES, d), F32)
    rows = min(CONV_ROWS, tm)
    for cs in range(0, d, CONV_COLS):
        for rb in range(0, tm, rows):
            acc = jnp.broadcast_to(b_ref[:, cs:cs + CONV_COLS], (rows, CONV_COLS))
            for r in range(SUBLANES):
                z = None
                for e in range(r, pad + width, SUBLANES):
                    if e < pad:
                        continue
                    blk = ext_ref[rb + e - r:rb + e - r + rows + SUBLANES, cs:cs + CONV_COLS]
                    term = blk * w_ref[e - pad:e - pad + 1, cs:cs + CONV_COLS]
                    z = term if z is None else z + term
                acc = acc + z[r:r + rows]
            c_ref[rb:rb + rows, cs:cs + CONV_COLS] = acc
    carry = ext_ref[tm:tm + CONV_HIST, :]
    ext_ref[0:CONV_HIST, :] = carry
    for rb in range(0, tm, rows):
        c = c_ref[rb:rb + rows, :]
        mu = jnp.mean(c, axis=-1, keepdims=True)
        xc = c - mu
        var = jnp.mean(xc * xc, axis=-1, keepdims=True)
        y = (xc * lax.rsqrt(var + EPS)) * g_ref[...] + beta_ref[...]
        o_ref[rb:rb + rows, :] = _silu(y).astype(o_ref.dtype)


def _conv_ln(u, state, w_dw, b_dw, ln_g, ln_b, *, batch, seq_len):
    m, d = u.shape
    width = w_dw.shape[0]
    assert width - 1 <= CONV_HIST
    tm = _row_tile(seq_len, 128)
    nt = seq_len // tm
    st = jnp.pad(state, ((0, 0), (CONV_HIST - (width - 1), 0), (0, 0)))
    w = jnp.pad(w_dw, ((0, CONV_HIST - width), (0, 0)))
    row = lambda a: a.reshape(1, d)
    vec_spec = pl.BlockSpec((1, d), lambda b, t: (0, 0))
    return pl.pallas_call(
        functools.partial(_conv_ln_body, width=width),
        grid=(batch, nt),
        in_specs=[pl.BlockSpec((tm, d), lambda b, t: (b * nt + t, 0)),
                  pl.BlockSpec((1, CONV_HIST, d), lambda b, t: (b, 0, 0)),
                  pl.BlockSpec((CONV_HIST, d), lambda b, t: (0, 0)),
                  vec_spec, vec_spec, vec_spec],
        out_specs=pl.BlockSpec((tm, d), lambda b, t: (b * nt + t, 0)),
        out_shape=jax.ShapeDtypeStruct((m, d), BF16),
        scratch_shapes=[pltpu.VMEM((CONV_HIST + tm + SUBLANES, d), F32), pltpu.VMEM((tm, d), F32)],
        compiler_params=_params(2),
        name="conv_ln",
    )(u, st, w, row(b_dw), row(ln_g), row(ln_b))


def _split3(x):
    h1 = x.astype(BF16)
    r1 = x - h1.astype(F32)
    h2 = r1.astype(BF16)
    h3 = (r1 - h2.astype(F32)).astype(BF16)
    return h1, h2, h3


def _exact_dot(dims, a_f32, b_bf16):
    return sum(lax.dot_general(piece, b_bf16, dims, preferred_element_type=F32)
               for piece in _split3(a_f32))


def _gla_body(proj_ref, wgu_ref, bias_ref, norm_ref, s0_ref, o_ref, sout_ref, s_ref, *, dk, dv):
    n_seq, c, _ = proj_ref.shape
    dkh, dvh = dk // GLA_HEADS, dv // GLA_HEADS
    step = pl.program_id(1)

    @pl.when(step == 0)
    def _():
        s_ref[...] = s0_ref[...]

    ri = lax.broadcasted_iota(jnp.int32, (c, c), 0)
    ci = lax.broadcasted_iota(jnp.int32, (c, c), 1)
    causal = ri >= ci
    tri = jnp.where(causal, 1.0, 0.0).astype(BF16)
    last_sel = jnp.where(lax.broadcasted_iota(jnp.int32, (c, LANES), 0) == c - 1, 1.0, 0.0).astype(BF16)
    tn_dims = (((0,), (0,)), ((), ()))
    for g in range(n_seq):
        gl = proj_ref[g, :, 2 * dk + 2 * dv:2 * dk + 2 * dv + LANES].astype(BF16)
        logit = jnp.dot(gl, wgu_ref[...], preferred_element_type=F32) + bias_ref[...]
        log_a = (jnp.minimum(logit, 0.0) - jnp.log1p(jnp.exp(-jnp.abs(logit)))) * (1.0 / GLA_TAU)
        b_all = sum(jnp.dot(tri, piece, preferred_element_type=F32) for piece in _split3(log_a))
        for h in range(GLA_HEADS):
            b = b_all[:, h * dkh:(h + 1) * dkh]
            b_last = b[c - 1:c, :]
            q = proj_ref[g, :, h * dkh:(h + 1) * dkh] * (dkh ** -0.5)
            k = proj_ref[g, :, dk + h * dkh:dk + (h + 1) * dkh]
            v = proj_ref[g, :, 2 * dk + h * dvh:2 * dk + (h + 1) * dvh].astype(BF16)
            r = proj_ref[g, :, 2 * dk + dv + h * dvh:2 * dk + dv + (h + 1) * dvh]
            qt = (q * jnp.exp(b)).astype(BF16)
            kt = (k * jnp.exp(-b)).astype(BF16)
            k_dec = (k * jnp.exp(b_last - b)).astype(BF16)
            s_old = s_ref[g, h]
            o_inter = jnp.dot(qt, s_old.astype(BF16), preferred_element_type=F32)
            a = lax.dot_general(qt, kt, (((1,), (1,)), ((), ())), preferred_element_type=F32)
            a = jnp.where(causal, a, 0.0).astype(BF16)
            o = o_inter + jnp.dot(a, v, preferred_element_type=F32)
            b_last_col = _exact_dot(tn_dims, b, last_sel)
            decay = jnp.exp(b_last_col[:, 0:1])
            s_ref[g, h] = decay * s_old + lax.dot_general(k_dec, v, tn_dims, preferred_element_type=F32)
            ms = jnp.mean(o * o, axis=-1, keepdims=True)
            o_n = (o * lax.rsqrt(ms + EPS)) * norm_ref[...]
            o_ref[g, :, h * dvh:(h + 1) * dvh] = (o_n * _silu(r)).astype(o_ref.dtype)

    @pl.when(step == pl.num_programs(1) - 1)
    def _():
        sout_ref[...] = s_ref[...]


def _gla(proj, wgu, bias, out_norm, state, *, batch, seq_len, dk, dv):
    m, width = proj.shape
    c = min(CHUNK, seq_len)
    nc = seq_len // c
    n_seq = 2 if batch % 2 == 0 else 1
    dkh, dvh = dk // GLA_HEADS, dv // GLA_HEADS
    rank = wgu.shape[0]
    assert rank <= LANES and 2 * dk + 2 * dv + LANES <= width
    wgu_pad = jnp.pad(wgu, ((0, LANES - rank), (0, 0))).astype(BF16)
    state_spec = pl.BlockSpec((n_seq, GLA_HEADS, dkh, dvh), lambda b, t: (b, 0, 0, 0))
    o, s_new = pl.pallas_call(
        functools.partial(_gla_body, dk=dk, dv=dv),
        grid=(batch // n_seq, nc),
        in_specs=[pl.BlockSpec((n_seq, c, width), lambda b, t: (b, t, 0)),
                  pl.BlockSpec(wgu_pad.shape, lambda b, t: (0, 0)),
                  pl.BlockSpec((1, dk), lambda b, t: (0, 0)),
                  pl.BlockSpec((1, dvh), lambda b, t: (0, 0)),
                  state_spec],
        out_specs=[pl.BlockSpec((n_seq, c, dv), lambda b, t: (b, t, 0)), state_spec],
        out_shape=[jax.ShapeDtypeStruct((batch, seq_len, dv), BF16),
                   jax.ShapeDtypeStruct((batch, GLA_HEADS, dkh, dvh), F32)],
        scratch_shapes=[pltpu.VMEM((n_seq, GLA_HEADS, dkh, dvh), F32)],
        compiler_params=_params(2),
        name="gla",
    )(proj.reshape(batch, seq_len, width), wgu_pad, bias.reshape(1, dk), out_norm.reshape(1, dvh),
      state)
    return o.reshape(m, dv), s_new


def _ffn_body(x_ref, gain_ref, wg_ref, wu_ref, cw_ref, cb_ref, wd_ref, st_ref, o_ref, ns_ref,
              xn_ref, *carry, seq_tile, tiles_per_seq):
    tm = x_ref.shape[0]
    tn = wg_ref.shape[1]
    nseq = tm // seq_tile
    i = pl.program_id(0)
    j = pl.program_id(1)

    @pl.when(j == 0)
    def _():
        x = x_ref[...]
        xn_ref[...] = _rms_to_bf16(x, gain_ref[...])
        o_ref[...] = x

    xn = xn_ref[...]
    g = jnp.dot(xn, wg_ref[...], preferred_element_type=F32)
    u = jnp.dot(xn, wu_ref[...], preferred_element_type=F32)
    if tiles_per_seq > 1:
        prev = jnp.where(i % tiles_per_seq == 0, st_ref[...], carry[0][j])
    else:
        prev = st_ref[...]
    g3 = g.reshape(nseq, seq_tile, tn)
    row = lax.broadcasted_iota(jnp.int32, (nseq, seq_tile, tn), 1)
    back1 = jnp.where(row == 0, prev[:, 1:2, :], pltpu.roll(g, 1, 0).reshape(nseq, seq_tile, tn))
    back2 = jnp.where(row == 0, prev[:, 0:1, :],
                      jnp.where(row == 1, prev[:, 1:2, :],
                                pltpu.roll(g, 2, 0).reshape(nseq, seq_tile, tn)))
    gc = (back2 * cw_ref[0:1, :] + back1 * cw_ref[1:2, :] + g3 * cw_ref[2:3, :]) + cb_ref[...]
    act = (_silu(gc).reshape(tm, tn) * u).astype(BF16)
    o_ref[...] += jnp.dot(act, wd_ref[...], preferred_element_type=F32)
    tail = g3[:, seq_tile - 2:seq_tile, :]
    ns_ref[...] = tail
    if tiles_per_seq > 1:
        carry[0][j] = tail


def _ffn(h, gain, w_up, conv_w, conv_b, w_down, layer, state, *, seq_len, tn=512):
    m, d = h.shape
    dff = w_down.shape[1]
    assert dff % tn == 0
    nj = dff // tn
    tm = _row_tile(m, 1024)
    seq_tile = min(seq_len, tm)
    tiles_per_seq = seq_len // seq_tile
    nseq = tm // seq_tile
    width = conv_w.shape[0]
    assert width == 3 and state.shape[1] == width - 1
    st_map = lambda i, j: (i // tiles_per_seq, 0, j)
    tail_map = lambda i, j: (i, 0, j)
    scratch = [pltpu.VMEM((tm, d), BF16)]
    if tiles_per_seq > 1:
        scratch.append(pltpu.VMEM((nj, nseq, width - 1, tn), F32))
    out, tails = pl.pallas_call(
        functools.partial(_ffn_body, seq_tile=seq_tile, tiles_per_seq=tiles_per_seq),
        grid=(m // tm, nj),
        in_specs=[pl.BlockSpec((tm, d), lambda i, j: (i, 0)),
                  pl.BlockSpec((1, d), lambda i, j: (0, 0)),
                  pl.BlockSpec((None, d, tn), lambda i, j: (layer, 0, j)),
                  pl.BlockSpec((None, d, tn), lambda i, j: (layer, 0, j + nj)),
                  pl.BlockSpec((width, tn), lambda i, j: (0, j)),
                  pl.BlockSpec((1, tn), lambda i, j: (0, j)),
                  pl.BlockSpec((None, tn, d), lambda i, j: (layer, j, 0)),
                  pl.BlockSpec((nseq, width - 1, tn), st_map)],
        out_specs=[pl.BlockSpec((tm, d), lambda i, j: (i, 0)),
                   pl.BlockSpec((nseq, width - 1, tn), tail_map)],
        out_shape=[jax.ShapeDtypeStruct((m, d), F32),
                   jax.ShapeDtypeStruct((m // seq_tile, width - 1, dff), F32)],
        scratch_shapes=scratch,
        compiler_params=_params(2),
        name="conv_ffn",
    )(h, gain.reshape(1, d), w_up, w_up, conv_w, conv_b.reshape(1, dff), w_down, state)
    return out, tails.reshape(-1, tiles_per_seq, width - 1, dff)[:, -1]


def _run_group(h, p, *, batch, seq_len, first_pos, states, weights):
    (cache_k, cache_v, st_conv, st_gla, st_ffn) = states
    w = weights
    m, d = h.shape
    depth = w["norm_mix"].shape[0]
    n_heads = w["a_w_o"].shape[1] // A_HEAD_DIM
    dkv = A_KV_HEADS * A_HEAD_DIM
    cos, sin = _rope_tables(first_pos + jnp.arange(seq_len))
    new_k, new_v, new_conv, new_gla, new_ffn = [], [], [], [], []
    for layer in range(depth):
        kind, slot = layer % 3, layer // 3
        if kind == 0:
            q, k, v = _attn_in(h, w["norm_mix"][layer], w["a_w_qkv"], slot, cos, sin,
                               w["a_q_norm"][slot], w["a_k_norm"][slot], seq_len=seq_len,
                               n_heads=n_heads)
            if cache_k is None:
                o = _attn_prompt(q, k, v, w["a_sinks"][slot], batch=batch, seq_len=seq_len,
                                 n_heads=n_heads)
                new_k.append(k.reshape(batch, seq_len, dkv)[:, seq_len - WINDOW:])
                new_v.append(v.reshape(batch, seq_len, dkv)[:, seq_len - WINDOW:])
            else:
                o, nk, nv = _attn_sample(q, k, v, cache_k[slot], cache_v[slot], w["a_sinks"][slot],
                                         batch=batch, seq_len=seq_len, n_heads=n_heads)
                new_k.append(nk)
                new_v.append(nv)
            mix_in, mix_w = o, w["a_w_o"]
        elif kind == 1:
            tn = 1024
            u = _linear(h, w["b_w_pw1"], slot, n_out=d, tm=1024, tn=tn, epilogue=_ep_glu,
                        out_dtype=F32, gain=w["norm_mix"][layer], w_offsets=(0, d // tn),
                        name="conf_pw1_glu")
            hist = w["b_w_dw"].shape[1] - 1
            if st_conv is None:
                conv_state = jnp.zeros((batch, hist, d), F32)
            else:
                conv_state = st_conv[slot]
            mix_in = _conv_ln(u, conv_state, w["b_w_dw"][slot], w["b_dw_bias"][slot],
                              w["b_ln_g"][slot], w["b_ln_b"][slot], batch=batch, seq_len=seq_len)
            u3 = u.reshape(batch, seq_len, d)
            if seq_len >= hist:
                new_conv.append(u3[:, seq_len - hist:])
            else:
                new_conv.append(jnp.concatenate([conv_state, u3], axis=1)[:, seq_len:])
            mix_w = w["b_w_pw2"]
        else:
            dk = w["c_w_gate_up"].shape[2]
            dv = w["c_w_o"].shape[1]
            proj = _linear(h, w["c_w_in"], slot, n_out=w["c_w_in"].shape[2], tm=1024, tn=GLA_IN_TN,
                           epilogue=_ep_plain, out_dtype=F32, gain=w["norm_mix"][layer],
                           name="gla_in")
            if st_gla is None:
                s0 = jnp.zeros((batch, GLA_HEADS, dk // GLA_HEADS, dv // GLA_HEADS), F32)
            else:
                s0 = st_gla[slot]
            mix_in, s_new = _gla(proj, w["c_w_gate_up"][slot], w["c_gate_bias"][slot],
                                 w["c_out_norm"][slot], s0, batch=batch, seq_len=seq_len, dk=dk, dv=dv)
            new_gla.append(s_new)
            mix_w = w["c_w_o"]
        h = _linear(mix_in, mix_w, slot, n_out=d, tm=512, tn=d, epilogue=_ep_residual,
                    out_dtype=F32, extras=[_residual_extra(h, 512, d)], name="mixer_out")
        dff = w["ffn_w_down"].shape[1]
        if st_ffn is None:
            ffn_state = jnp.zeros((batch, w["ffn_conv_w"].shape[1] - 1, dff), F32)
        else:
            ffn_state = st_ffn[layer]
        h, ns = _ffn(h, w["norm_ffn"][layer], w["ffn_w_up"], w["ffn_conv_w"][layer],
                     w["ffn_conv_b"][layer], w["ffn_w_down"], layer, ffn_state, seq_len=seq_len)
        new_ffn.append(ns)
        tm = _row_tile(m, 512)
        ple_dim = p.shape[-1]
        h = _linear(h, w["ple_w_gate"], layer, n_out=d, tm=512, tn=d, epilogue=_ep_ple,
                    out_dtype=F32, gain=w["ple_norm"][layer],
                    extras=[(p, (None, tm, ple_dim), lambda i, j, layer=layer: (layer, i, 0)),
                            (w["ple_w_proj"], (None, ple_dim, d),
                             lambda i, j, layer=layer: (layer, 0, j))],
                    name="ple")
    kv_shape = (len(new_k), batch, WINDOW, A_KV_HEADS, A_HEAD_DIM)
    return (h.reshape(batch, seq_len, d),
            jnp.stack(new_k).reshape(kv_shape), jnp.stack(new_v).reshape(kv_shape),
            jnp.stack(new_conv), jnp.stack(new_gla), jnp.stack(new_ffn))


def kernel(x_prompt, x_sample, p_prompt, p_sample, cache_k_a, cache_v_a, state_conv_b, state_gla_c, state_ffn_conv, norm_mix, norm_ffn, a_w_qkv, a_q_norm, a_k_norm, a_sinks, a_w_o, b_w_pw1, b_w_dw, b_dw_bias, b_ln_g, b_ln_b, b_w_pw2, c_w_in, c_w_gate_up, c_gate_bias, c_out_norm, c_w_o, ffn_w_up, ffn_conv_w, ffn_conv_b, ffn_w_down, ple_w_proj, ple_norm, ple_w_gate):
    bp, lp, d = x_prompt.shape
    bs, ls, _ = x_sample.shape
    depth = norm_mix.shape[0]
    gla_pad = (-c_w_in.shape[2]) % GLA_IN_TN
    weights = dict(
        norm_mix=norm_mix, norm_ffn=norm_ffn,
        a_w_qkv=a_w_qkv.astype(BF16), a_q_norm=a_q_norm, a_k_norm=a_k_norm, a_sinks=a_sinks,
        a_w_o=a_w_o.astype(BF16),
        b_w_pw1=b_w_pw1.astype(BF16), b_w_dw=b_w_dw, b_dw_bias=b_dw_bias, b_ln_g=b_ln_g,
        b_ln_b=b_ln_b, b_w_pw2=b_w_pw2.astype(BF16),
        c_w_in=jnp.pad(c_w_in, ((0, 0), (0, 0), (0, gla_pad))).astype(BF16),
        c_w_gate_up=c_w_gate_up, c_gate_bias=c_gate_bias, c_out_norm=c_out_norm,
        c_w_o=c_w_o.astype(BF16),
        ffn_w_up=ffn_w_up.astype(BF16), ffn_conv_w=ffn_conv_w, ffn_conv_b=ffn_conv_b,
        ffn_w_down=ffn_w_down.astype(BF16),
        ple_w_proj=ple_w_proj.astype(BF16), ple_norm=ple_norm, ple_w_gate=ple_w_gate.astype(BF16),
    )
    prompt = _run_group(x_prompt.reshape(bp * lp, d), p_prompt.reshape(depth, bp * lp, -1),
                        batch=bp, seq_len=lp, first_pos=0,
                        states=(None, None, None, None, None), weights=weights)
    sample = _run_group(x_sample.reshape(bs * ls, d), p_sample.reshape(depth, bs * ls, -1),
                        batch=bs, seq_len=ls, first_pos=PAST_LEN,
                        states=(cache_k_a, cache_v_a, state_conv_b, state_gla_c, state_ffn_conv),
                        weights=weights)
    return (prompt[0], sample[0]) + prompt[1:] + sample[1:]
```

```python
import functools

import jax
import jax.numpy as jnp
from jax import lax
from jax.experimental import pallas as pl
from jax.experimental.pallas import tpu as pltpu

F32 = jnp.float32
BF16 = jnp.bfloat16

EPS = 1e-6
CHUNK = 64
WINDOW = 128
A_HEAD_DIM = 64
A_KV_HEADS = 4
ROPE_THETA = 10000.0
PAST_LEN = 2048
GLA_HEADS = 4
GLA_TAU = 16.0
GLA_IN_TN = 1280
LANES = 128
SUBLANES = 8
VMEM_LIMIT_BYTES = 56 * 2**20


def _params(n_axes):
    return pltpu.CompilerParams(dimension_semantics=("arbitrary",) * n_axes,
                                vmem_limit_bytes=VMEM_LIMIT_BYTES)


def _sigmoid(x):
    return 1.0 / (1.0 + jnp.exp(-x))


def _silu(x):
    return x * _sigmoid(x)


def _rms_to_bf16(x, g):
    ms = jnp.mean(x * x, axis=-1, keepdims=True)
    return ((x * lax.rsqrt(ms + EPS)) * g).astype(BF16)


def _row_tile(m, want):
    t = min(m, want)
    assert m % t == 0, (m, t)
    return t


def _linear_body(*refs, has_norm, n_w, n_extra, epilogue):
    it = iter(refs)
    x_ref = next(it)
    g_ref = next(it) if has_norm else None
    w_refs = [next(it) for _ in range(n_w)]
    extra_refs = [next(it) for _ in range(n_extra)]
    o_ref = next(it)
    if has_norm:
        xn_ref = next(it)

        @pl.when(pl.program_id(1) == 0)
        def _():
            xn_ref[...] = _rms_to_bf16(x_ref[...], g_ref[...])

        lhs = xn_ref[...]
    else:
        lhs = x_ref[...].astype(BF16)
    accs = [jnp.dot(lhs, w[...], preferred_element_type=F32) for w in w_refs]
    o_ref[...] = epilogue(accs, extra_refs, x_ref).astype(o_ref.dtype)


def _linear(x, w, layer, *, n_out, tm, tn, epilogue, out_dtype, name, gain=None,
            w_offsets=(0,), extras=()):
    m, k = x.shape
    tm = _row_tile(m, tm)
    assert n_out % tn == 0
    in_specs = [pl.BlockSpec((tm, k), lambda i, j: (i, 0))]
    args = [x]
    if gain is not None:
        in_specs.append(pl.BlockSpec((1, k), lambda i, j: (0, 0)))
        args.append(gain.reshape(1, k))
    for off in w_offsets:
        in_specs.append(pl.BlockSpec((None, k, tn), lambda i, j, off=off: (layer, 0, j + off)))
        args.append(w)
    for arr, shape, imap in extras:
        in_specs.append(pl.BlockSpec(shape, imap))
        args.append(arr)
    body = functools.partial(_linear_body, has_norm=gain is not None, n_w=len(w_offsets),
                             n_extra=len(extras), epilogue=epilogue)
    return pl.pallas_call(
        body,
        grid=(m // tm, n_out // tn),
        in_specs=in_specs,
        out_specs=pl.BlockSpec((tm, tn), lambda i, j: (i, j)),
        out_shape=jax.ShapeDtypeStruct((m, n_out), out_dtype),
        scratch_shapes=[pltpu.VMEM((tm, k), BF16)] if gain is not None else [],
        compiler_params=_params(2),
        name=name,
    )(*args)


def _ep_plain(accs, extras, x_ref):
    return accs[0]


def _ep_residual(accs, extras, x_ref):
    return extras[0][...] + accs[0]


def _ep_glu(accs, extras, x_ref):
    return accs[0] * _sigmoid(accs[1])


def _ep_ple(accs, extras, x_ref):
    p_ref, wp_ref = extras
    proj = jnp.dot(p_ref[...].astype(BF16), wp_ref[...], preferred_element_type=F32)
    return x_ref[...] + proj * _sigmoid(accs[0])


def _residual_extra(h, tm, tn):
    return (h, (_row_tile(h.shape[0], tm), tn), lambda i, j: (i, j))


def _rope_tables(pos):
    half = A_HEAD_DIM // 2
    inv = 1.0 / (ROPE_THETA ** (jnp.arange(half, dtype=F32) / half))
    ang = pos.astype(F32)[:, None] * inv[None, :]
    cos, sin = jnp.cos(ang), jnp.sin(ang)
    return jnp.tile(cos, (1, 4)), jnp.tile(jnp.concatenate([-sin, sin], axis=1), (1, 2))


def _attn_in_body(x_ref, gain_ref, w_ref, cos_ref, sin_ref, qg_ref, kg_ref, q_ref, k_ref, v_ref, *,
                  n_q_pairs, n_k_pairs):
    xn = _rms_to_bf16(x_ref[...], gain_ref[...])
    cos = cos_ref[...]
    sin = sin_ref[...]
    lane = lax.broadcasted_iota(jnp.int32, cos.shape, 1)
    left = lane < A_HEAD_DIM
    first_half = (lane & (A_HEAD_DIM // 2)) == 0
    n_pairs = w_ref.shape[1] // LANES
    for c in range(0, n_pairs, 2):
        acc = jnp.dot(xn, w_ref[:, c * LANES:(c + 2) * LANES], preferred_element_type=F32)
        for p in (c, c + 1):
            x = acc[:, (p - c) * LANES:(p - c + 1) * LANES]
            if p >= n_q_pairs + n_k_pairs:
                pv = p - n_q_pairs - n_k_pairs
                v_ref[:, pv * LANES:(pv + 1) * LANES] = x
                continue
            is_q = p < n_q_pairs
            sq = x * x
            s_left = jnp.sum(jnp.where(left, sq, 0.0), axis=-1, keepdims=True)
            s_right = jnp.sum(jnp.where(left, 0.0, sq), axis=-1, keepdims=True)
            ms = jnp.where(left, s_left, s_right) * (1.0 / A_HEAD_DIM)
            y = (x * lax.rsqrt(ms + EPS)) * (qg_ref[...] if is_q else kg_ref[...])
            partner = jnp.where(first_half, pltpu.roll(y, LANES - A_HEAD_DIM // 2, 1),
                                pltpu.roll(y, A_HEAD_DIM // 2, 1))
            out = y * cos + partner * sin
            if is_q:
                q_ref[p] = (out * (A_HEAD_DIM ** -0.5)).astype(BF16)
            else:
                pk = p - n_q_pairs
                k_ref[:, pk * LANES:(pk + 1) * LANES] = out


def _attn_in(h, gain, w_qkv, slot, cos, sin, q_norm, k_norm, *, seq_len, n_heads):
    m, d = h.shape
    width = w_qkv.shape[2]
    tm = _row_tile(m, 512)
    if seq_len < tm:
        cos = jnp.tile(cos, (tm // seq_len, 1))
        sin = jnp.tile(sin, (tm // seq_len, 1))
    n_tab = cos.shape[0] // tm
    n_q_pairs = n_heads * A_HEAD_DIM // LANES
    dk = A_KV_HEADS * A_HEAD_DIM
    assert width == n_q_pairs * LANES + 2 * dk and (width // LANES) % 2 == 0
    body = functools.partial(_attn_in_body, n_q_pairs=n_q_pairs, n_k_pairs=dk // LANES)
    tab_spec = pl.BlockSpec((tm, LANES), lambda i: (i % n_tab, 0))
    gain_spec = pl.BlockSpec((1, LANES), lambda i: (0, 0))
    kv_spec = pl.BlockSpec((tm, dk), lambda i: (i, 0))
    return pl.pallas_call(
        body,
        grid=(m // tm,),
        in_specs=[pl.BlockSpec((tm, d), lambda i: (i, 0)),
                  pl.BlockSpec((1, d), lambda i: (0, 0)),
                  pl.BlockSpec((None, d, width), lambda i: (slot, 0, 0)),
                  tab_spec, tab_spec, gain_spec, gain_spec],
        out_specs=[pl.BlockSpec((n_q_pairs, tm, LANES), lambda i: (0, i, 0)), kv_spec, kv_spec],
        out_shape=[jax.ShapeDtypeStruct((n_q_pairs, m, LANES), BF16),
                   jax.ShapeDtypeStruct((m, dk), F32), jax.ShapeDtypeStruct((m, dk), F32)],
        compiler_params=_params(1),
        name="attn_in",
    )(h, gain.reshape(1, d), w_qkv, cos, sin, jnp.tile(q_norm, 2).reshape(1, LANES),
      jnp.tile(k_norm, 2).reshape(1, LANES))


def _even_odd_stack(x_all, g):
    pair = x_all[:, (g // 2) * LANES:(g // 2 + 1) * LANES]
    swapped = pltpu.roll(pair, A_HEAD_DIM, 1)
    low = lax.broadcasted_iota(jnp.int32, pair.shape, 1) < A_HEAD_DIM
    if g % 2 == 0:
        lo, hi = jnp.where(low, pair, 0.0), jnp.where(low, 0.0, swapped)
    else:
        lo, hi = jnp.where(low, swapped, 0.0), jnp.where(low, 0.0, pair)
    return jnp.concatenate([lo, hi], axis=0).astype(BF16)


def _attn_groups(q_ref, k_all, v_all, sinks_ref, bias_ref, o_ref, n_heads):
    tq = q_ref.shape[1]
    s_len = k_all.shape[0]
    ppg = n_heads // A_KV_HEADS // 2
    for g in range(A_KV_HEADS):
        k_cat = _even_odd_stack(k_all, g)
        v_cat = _even_odd_stack(v_all, g)
        q = q_ref[g * ppg:(g + 1) * ppg].reshape(ppg * tq, LANES)
        s = lax.dot_general(q, k_cat, (((1,), (1,)), ((), ())), preferred_element_type=F32)
        s = s + bias_ref[...]
        probs = []
        for p in range(ppg):
            halves = []
            for half in range(2):
                sh = s[p * tq:(p + 1) * tq, half * s_len:(half + 1) * s_len]
                sink = sinks_ref[2 * (g * ppg + p) + half]
                mx = jnp.maximum(jnp.max(sh, axis=-1, keepdims=True), sink)
                e = jnp.exp(sh - mx)
                denom = jnp.sum(e, axis=-1, keepdims=True) + jnp.exp(sink - mx)
                halves.append((e * (1.0 / denom)).astype(BF16))
            probs.append(jnp.concatenate(halves, axis=1))
        o = jnp.dot(jnp.concatenate(probs, axis=0), v_cat, preferred_element_type=F32)
        for p in range(ppg):
            col = (g * ppg + p) * LANES
            o_ref[:, col:col + LANES] = o[p * tq:(p + 1) * tq].astype(o_ref.dtype)


def _attn_prompt_body(sinks_ref, bias_ref, q_ref, kp_ref, kc_ref, vp_ref, vc_ref, o_ref, *, n_heads):
    k_all = jnp.concatenate([kp_ref[...], kc_ref[...]], axis=0)
    v_all = jnp.concatenate([vp_ref[...], vc_ref[...]], axis=0)
    _attn_groups(q_ref, k_all, v_all, sinks_ref, bias_ref, o_ref, n_heads)


def _band_bias(tq, rows):
    s_len = WINDOW + tq
    q_chunk = (jnp.arange(rows)[:, None] % tq) // CHUNK
    k_chunk = (jnp.arange(2 * s_len)[None, :] % s_len) // CHUNK - WINDOW // CHUNK
    band = (k_chunk <= q_chunk) & (k_chunk >= q_chunk - WINDOW // CHUNK)
    visible = jnp.stack([band & (k_chunk >= 0), band])
    return jnp.where(visible, 0.0, -jnp.inf).astype(F32)


def _attn_prompt(q, k, v, sinks, *, batch, seq_len, n_heads):
    n_pairs, m, _ = q.shape
    tq = WINDOW
    nq = seq_len // tq
    dk = A_KV_HEADS * A_HEAD_DIM
    d_out = n_heads * A_HEAD_DIM
    rows = n_pairs // A_KV_HEADS * tq
    bias = _band_bias(tq, rows)

    def prev_row(b, i):
        return jnp.maximum(b * nq + i - 1, 0)

    prev_spec = pl.BlockSpec((WINDOW, dk), lambda b, i: (prev_row(b, i), 0))
    cur_spec = pl.BlockSpec((tq, dk), lambda b, i: (b * nq + i, 0))
    return pl.pallas_call(
        functools.partial(_attn_prompt_body, n_heads=n_heads),
        grid=(batch, nq),
        in_specs=[
            pl.BlockSpec(memory_space=pltpu.SMEM),
            pl.BlockSpec((None,) + bias.shape[1:], lambda b, i: (jnp.minimum(i, 1), 0, 0)),
            pl.BlockSpec((n_pairs, tq, LANES), lambda b, i: (0, b * nq + i, 0)),
            prev_spec, cur_spec, prev_spec, cur_spec,
        ],
        out_specs=pl.BlockSpec((tq, d_out), lambda b, i: (b * nq + i, 0)),
        out_shape=jax.ShapeDtypeStruct((m, d_out), BF16),
        compiler_params=_params(2),
        name="attn_prompt",
    )(sinks, bias, q, k, k, v, v)


def _attn_sample_body(sinks_ref, bias_ref, q_ref, kn_ref, vn_ref, ck_ref, cv_ref, o_ref, nk_ref,
                      nv_ref, *, n_heads):
    t = q_ref.shape[1]
    dk = kn_ref.shape[1]
    k_new, v_new = kn_ref[...], vn_ref[...]
    nk_ref[0, 0:WINDOW - t, :] = ck_ref[0, t:WINDOW, :]
    nk_ref[0, WINDOW - t:WINDOW, :] = k_new
    nv_ref[0, 0:WINDOW - t, :] = cv_ref[0, t:WINDOW, :]
    nv_ref[0, WINDOW - t:WINDOW, :] = v_new
    s_len = bias_ref.shape[1] // 2
    zeros = jnp.zeros((s_len - WINDOW - t, dk), F32)
    k_all = jnp.concatenate([ck_ref[0], k_new, zeros], axis=0)
    v_all = jnp.concatenate([cv_ref[0], v_new, zeros], axis=0)
    _attn_groups(q_ref, k_all, v_all, sinks_ref, bias_ref, o_ref, n_heads)


def _attn_sample(q, k, v, cache_k, cache_v, sinks, *, batch, seq_len, n_heads):
    n_pairs, m, _ = q.shape
    dk = A_KV_HEADS * A_HEAD_DIM
    d_out = n_heads * A_HEAD_DIM
    rows = n_pairs // A_KV_HEADS * seq_len
    s_len = -(-(WINDOW + seq_len) // LANES) * LANES
    visible = (jnp.arange(2 * s_len)[None, :] % s_len) < WINDOW + seq_len
    bias = jnp.broadcast_to(jnp.where(visible, 0.0, -jnp.inf).astype(F32), (rows, 2 * s_len))
    cache_spec = pl.BlockSpec((1, WINDOW, dk), lambda b: (b, 0, 0))
    new_spec = pl.BlockSpec((seq_len, dk), lambda b: (b, 0))
    return pl.pallas_call(
        functools.partial(_attn_sample_body, n_heads=n_heads),
        grid=(batch,),
        in_specs=[
            pl.BlockSpec(memory_space=pltpu.SMEM),
            pl.BlockSpec(bias.shape, lambda b: (0, 0)),
            pl.BlockSpec((n_pairs, seq_len, LANES), lambda b: (0, b, 0)),
            new_spec, new_spec, cache_spec, cache_spec,
        ],
        out_specs=[pl.BlockSpec((seq_len, d_out), lambda b: (b, 0)), cache_spec, cache_spec],
        out_shape=[jax.ShapeDtypeStruct((m, d_out), BF16),
                   jax.ShapeDtypeStruct((batch, WINDOW, dk), F32),
                   jax.ShapeDtypeStruct((batch, WINDOW, dk), F32)],
        compiler_params=_params(1),
        name="attn_sample",
    )(sinks, bias, q, k, v, cache_k.reshape(batch, WINDOW, dk), cache_v.reshape(batch, WINDOW, dk))


CONV_HIST = 32
CONV_ROWS = 128
CONV_COLS = 128


def _conv_ln_body(u_ref, st_ref, w_ref, b_ref, g_ref, beta_ref, o_ref, ext_ref, c_ref, *, width):
    tm, d = u_ref.shape
    pad = CONV_HIST - (width - 1)

    @pl.when(pl.program_id(1) == 0)
    def _():
        ext_ref[0:CONV_HIST, :] = st_ref[0]

    ext_ref[CONV_HIST:CONV_HIST + tm, :] = u_ref[...]
    ext_ref[CONV_HIST + tm:CONV_HIST + tm + SUBLANES, :] = jnp.zeros((SUBLANES, d), F32)
    rows = min(CONV_ROWS, tm)
    for cs in range(0, d, CONV_COLS):
        for rb in range(0, tm, rows):
            acc = jnp.broadcast_to(b_ref[:, cs:cs + CONV_COLS], (rows, CONV_COLS))
            for r in range(SUBLANES):
                z = None
                for e in range(r, pad + width, SUBLANES):
                    if e < pad:
                        continue
                    blk = ext_ref[rb + e - r:rb + e - r + rows + SUBLANES, cs:cs + CONV_COLS]
                    term = blk * w_ref[e - pad:e - pad + 1, cs:cs + CONV_COLS]
                    z = term if z is None else z + term
                acc = acc + z[r:r + rows]
            c_ref[rb:rb + rows, cs:cs + CONV_COLS] = acc
    carry = ext_ref[tm:tm + CONV_HIST, :]
    ext_ref[0:CONV_HIST, :] = carry
    for rb in range(0, tm, rows):
        c = c_ref[rb:rb + rows, :]
        mu = jnp.mean(c, axis=-1, keepdims=True)
        xc = c - mu
        var = jnp.mean(xc * xc, axis=-1, keepdims=True)
        y = (xc * lax.rsqrt(var + EPS)) * g_ref[...] + beta_ref[...]
        o_ref[rb:rb + rows, :] = _silu(y).astype(o_ref.dtype)


def _conv_ln(u, state, w_dw, b_dw, ln_g, ln_b, *, batch, seq_len):
    m, d = u.shape
    width = w_dw.shape[0]
    assert width - 1 <= CONV_HIST
    tm = _row_tile(seq_len, 256)
    nt = seq_len // tm
    st = jnp.pad(state, ((0, 0), (CONV_HIST - (width - 1), 0), (0, 0)))
    w = jnp.pad(w_dw, ((0, CONV_HIST - width), (0, 0)))
    row = lambda a: a.reshape(1, d)
    vec_spec = pl.BlockSpec((1, d), lambda b, t: (0, 0))
    return pl.pallas_call(
        functools.partial(_conv_ln_body, width=width),
        grid=(batch, nt),
        in_specs=[pl.BlockSpec((tm, d), lambda b, t: (b * nt + t, 0)),
                  pl.BlockSpec((1, CONV_HIST, d), lambda b, t: (b, 0, 0)),
                  pl.BlockSpec((CONV_HIST, d), lambda b, t: (0, 0)),
                  vec_spec, vec_spec, vec_spec],
        out_specs=pl.BlockSpec((tm, d), lambda b, t: (b * nt + t, 0)),
        out_shape=jax.ShapeDtypeStruct((m, d), BF16),
        scratch_shapes=[pltpu.VMEM((CONV_HIST + tm + SUBLANES, d), F32), pltpu.VMEM((tm, d), F32)],
        compiler_params=_params(2),
        name="conv_ln",
    )(u, st, w, row(b_dw), row(ln_g), row(ln_b))


def _split3(x):
    h1 = x.astype(BF16)
    r1 = x - h1.astype(F32)
    h2 = r1.astype(BF16)
    h3 = (r1 - h2.astype(F32)).astype(BF16)
    return h1, h2, h3


def _exact_dot(dims, a_f32, b_bf16):
    return sum(lax.dot_general(piece, b_bf16, dims, preferred_element_type=F32)
               for piece in _split3(a_f32))


def _gla_body(proj_ref, wgu_ref, bias_ref, norm_ref, s0_ref, o_ref, sout_ref, s_ref, *, dk, dv):
    n_seq, c, _ = proj_ref.shape
    dkh, dvh = dk // GLA_HEADS, dv // GLA_HEADS
    step = pl.program_id(1)

    @pl.when(step == 0)
    def _():
        s_ref[...] = s0_ref[...]

    ri = lax.broadcasted_iota(jnp.int32, (c, c), 0)
    ci = lax.broadcasted_iota(jnp.int32, (c, c), 1)
    causal = ri >= ci
    tri = jnp.where(causal, 1.0, 0.0).astype(BF16)
    last_sel = jnp.where(lax.broadcasted_iota(jnp.int32, (c, LANES), 0) == c - 1, 1.0, 0.0).astype(BF16)
    tn_dims = (((0,), (0,)), ((), ()))
    for g in range(n_seq):
        gl = proj_ref[g, :, 2 * dk + 2 * dv:2 * dk + 2 * dv + LANES].astype(BF16)
        logit = jnp.dot(gl, wgu_ref[...], preferred_element_type=F32) + bias_ref[...]
        log_a = (jnp.minimum(logit, 0.0) - jnp.log1p(jnp.exp(-jnp.abs(logit)))) * (1.0 / GLA_TAU)
        b_all = sum(jnp.dot(tri, piece, preferred_element_type=F32) for piece in _split3(log_a))
        for h in range(GLA_HEADS):
            b = b_all[:, h * dkh:(h + 1) * dkh]
            b_last = b[c - 1:c, :]
            q = proj_ref[g, :, h * dkh:(h + 1) * dkh] * (dkh ** -0.5)
            k = proj_ref[g, :, dk + h * dkh:dk + (h + 1) * dkh]
            v = proj_ref[g, :, 2 * dk + h * dvh:2 * dk + (h + 1) * dvh].astype(BF16)
            r = proj_ref[g, :, 2 * dk + dv + h * dvh:2 * dk + dv + (h + 1) * dvh]
            qt = (q * jnp.exp(b)).astype(BF16)
            kt = (k * jnp.exp(-b)).astype(BF16)
            k_dec = (k * jnp.exp(b_last - b)).astype(BF16)
            s_old = s_ref[g, h]
            o_inter = jnp.dot(qt, s_old.astype(BF16), preferred_element_type=F32)
            a = lax.dot_general(qt, kt, (((1,), (1,)), ((), ())), preferred_element_type=F32)
            a = jnp.where(causal, a, 0.0).astype(BF16)
            o = o_inter + jnp.dot(a, v, preferred_element_type=F32)
            decay = jnp.transpose(jnp.broadcast_to(jnp.exp(b_last), (SUBLANES, dkh)))[:, 0:1]
            s_ref[g, h] = decay * s_old + lax.dot_general(k_dec, v, tn_dims, preferred_element_type=F32)
            ms = jnp.mean(o * o, axis=-1, keepdims=True)
            o_n = (o * lax.rsqrt(ms + EPS)) * norm_ref[...]
            o_ref[g, :, h * dvh:(h + 1) * dvh] = (o_n * _silu(r)).astype(o_ref.dtype)

    @pl.when(step == pl.num_programs(1) - 1)
    def _():
        sout_ref[...] = s_ref[...]


def _gla(proj, wgu, bias, out_norm, state, *, batch, seq_len, dk, dv):
    m, width = proj.shape
    c = min(CHUNK, seq_len)
    nc = seq_len // c
    n_seq = 2 if batch % 2 == 0 else 1
    dkh, dvh = dk // GLA_HEADS, dv // GLA_HEADS
    rank = wgu.shape[0]
    assert rank <= LANES and 2 * dk + 2 * dv + LANES <= width
    wgu_pad = jnp.pad(wgu, ((0, LANES - rank), (0, 0))).astype(BF16)
    state_spec = pl.BlockSpec((n_seq, GLA_HEADS, dkh, dvh), lambda b, t: (b, 0, 0, 0))
    o, s_new = pl.pallas_call(
        functools.partial(_gla_body, dk=dk, dv=dv),
        grid=(batch // n_seq, nc),
        in_specs=[pl.BlockSpec((n_seq, c, width), lambda b, t: (b, t, 0)),
                  pl.BlockSpec(wgu_pad.shape, lambda b, t: (0, 0)),
                  pl.BlockSpec((1, dk), lambda b, t: (0, 0)),
                  pl.BlockSpec((1, dvh), lambda b, t: (0, 0)),
                  state_spec],
        out_specs=[pl.BlockSpec((n_seq, c, dv), lambda b, t: (b, t, 0)), state_spec],
        out_shape=[jax.ShapeDtypeStruct((batch, seq_len, dv), BF16),
                   jax.ShapeDtypeStruct((batch, GLA_HEADS, dkh, dvh), F32)],
        scratch_shapes=[pltpu.VMEM((n_seq, GLA_HEADS, dkh, dvh), F32)],
        compiler_params=_params(2),
        name="gla",
    )(proj.reshape(batch, seq_len, width), wgu_pad, bias.reshape(1, dk), out_norm.reshape(1, dvh),
      state)
    return o.reshape(m, dv), s_new


def _ffn_body(x_ref, gain_ref, wg_ref, wu_ref, cw_ref, cb_ref, wd_ref, st_ref, o_ref, ns_ref,
              xn_ref, *carry, seq_tile, tiles_per_seq):
    tm = x_ref.shape[0]
    tn = wg_ref.shape[1]
    nseq = tm // seq_tile
    i = pl.program_id(0)
    j = pl.program_id(1)

    @pl.when(j == 0)
    def _():
        x = x_ref[...]
        xn_ref[...] = _rms_to_bf16(x, gain_ref[...])
        o_ref[...] = x

    xn = xn_ref[...]
    g = jnp.dot(xn, wg_ref[...], preferred_element_type=F32)
    u = jnp.dot(xn, wu_ref[...], preferred_element_type=F32)
    if tiles_per_seq > 1:
        prev = jnp.where(i % tiles_per_seq == 0, st_ref[...], carry[0][j])
    else:
        prev = st_ref[...]
    g3 = g.reshape(nseq, seq_tile, tn)
    row = lax.broadcasted_iota(jnp.int32, (nseq, seq_tile, tn), 1)
    back1 = jnp.where(row == 0, prev[:, 1:2, :], pltpu.roll(g, 1, 0).reshape(nseq, seq_tile, tn))
    back2 = jnp.where(row == 0, prev[:, 0:1, :],
                      jnp.where(row == 1, prev[:, 1:2, :],
                                pltpu.roll(g, 2, 0).reshape(nseq, seq_tile, tn)))
    gc = (back2 * cw_ref[0:1, :] + back1 * cw_ref[1:2, :] + g3 * cw_ref[2:3, :]) + cb_ref[...]
    act = (_silu(gc).reshape(tm, tn) * u).astype(BF16)
    o_ref[...] += jnp.dot(act, wd_ref[...], preferred_element_type=F32)
    tail = g3[:, seq_tile - 2:seq_tile, :]
    ns_ref[...] = tail
    if tiles_per_seq > 1:
        carry[0][j] = tail


def _ffn(h, gain, w_up, conv_w, conv_b, w_down, layer, state, *, seq_len, tn=512):
    m, d = h.shape
    dff = w_down.shape[1]
    assert dff % tn == 0
    nj = dff // tn
    tm = _row_tile(m, 1024)
    seq_tile = min(seq_len, tm)
    tiles_per_seq = seq_len // seq_tile
    nseq = tm // seq_tile
    width = conv_w.shape[0]
    assert width == 3 and state.shape[1] == width - 1
    st_map = lambda i, j: (i // tiles_per_seq, 0, j)
    tail_map = lambda i, j: (i, 0, j)
    scratch = [pltpu.VMEM((tm, d), BF16)]
    if tiles_per_seq > 1:
        scratch.append(pltpu.VMEM((nj, nseq, width - 1, tn), F32))
    out, tails = pl.pallas_call(
        functools.partial(_ffn_body, seq_tile=seq_tile, tiles_per_seq=tiles_per_seq),
        grid=(m // tm, nj),
        in_specs=[pl.BlockSpec((tm, d), lambda i, j: (i, 0)),
                  pl.BlockSpec((1, d), lambda i, j: (0, 0)),
                  pl.BlockSpec((None, d, tn), lambda i, j: (layer, 0, j)),
                  pl.BlockSpec((None, d, tn), lambda i, j: (layer, 0, j + nj)),
                  pl.BlockSpec((width, tn), lambda i, j: (0, j)),
                  pl.BlockSpec((1, tn), lambda i, j: (0, j)),
                  pl.BlockSpec((None, tn, d), lambda i, j: (layer, j, 0)),
                  pl.BlockSpec((nseq, width - 1, tn), st_map)],
        out_specs=[pl.BlockSpec((tm, d), lambda i, j: (i, 0)),
                   pl.BlockSpec((nseq, width - 1, tn), tail_map)],
        out_shape=[jax.ShapeDtypeStruct((m, d), F32),
                   jax.ShapeDtypeStruct((m // seq_tile, width - 1, dff), F32)],
        scratch_shapes=scratch,
        compiler_params=_params(2),
        name="conv_ffn",
    )(h, gain.reshape(1, d), w_up, w_up, conv_w, conv_b.reshape(1, dff), w_down, state)
    return out, tails.reshape(-1, tiles_per_seq, width - 1, dff)[:, -1]


def _run_group(h, p, *, batch, seq_len, first_pos, states, weights):
    (cache_k, cache_v, st_conv, st_gla, st_ffn) = states
    w = weights
    m, d = h.shape
    depth = w["norm_mix"].shape[0]
    n_heads = w["a_w_o"].shape[1] // A_HEAD_DIM
    dkv = A_KV_HEADS * A_HEAD_DIM
    cos, sin = _rope_tables(first_pos + jnp.arange(seq_len))
    new_k, new_v, new_conv, new_gla, new_ffn = [], [], [], [], []
    for layer in range(depth):
        kind, slot = layer % 3, layer // 3
        if kind == 0:
            q, k, v = _attn_in(h, w["norm_mix"][layer], w["a_w_qkv"], slot, cos, sin,
                               w["a_q_norm"][slot], w["a_k_norm"][slot], seq_len=seq_len,
                               n_heads=n_heads)
            if cache_k is None:
                o = _attn_prompt(q, k, v, w["a_sinks"][slot], batch=batch, seq_len=seq_len,
                                 n_heads=n_heads)
                new_k.append(k.reshape(batch, seq_len, dkv)[:, seq_len - WINDOW:])
                new_v.append(v.reshape(batch, seq_len, dkv)[:, seq_len - WINDOW:])
            else:
                o, nk, nv = _attn_sample(q, k, v, cache_k[slot], cache_v[slot], w["a_sinks"][slot],
                                         batch=batch, seq_len=seq_len, n_heads=n_heads)
                new_k.append(nk)
                new_v.append(nv)
            mix_in, mix_w = o, w["a_w_o"]
        elif kind == 1:
            tn = 1024
            u = _linear(h, w["b_w_pw1"], slot, n_out=d, tm=1024, tn=tn, epilogue=_ep_glu,
                        out_dtype=F32, gain=w["norm_mix"][layer], w_offsets=(0, d // tn),
                        name="conf_pw1_glu")
            hist = w["b_w_dw"].shape[1] - 1
            if st_conv is None:
                conv_state = jnp.zeros((batch, hist, d), F32)
            else:
                conv_state = st_conv[slot]
            mix_in = _conv_ln(u, conv_state, w["b_w_dw"][slot], w["b_dw_bias"][slot],
                              w["b_ln_g"][slot], w["b_ln_b"][slot], batch=batch, seq_len=seq_len)
            u3 = u.reshape(batch, seq_len, d)
            if seq_len >= hist:
                new_conv.append(u3[:, seq_len - hist:])
            else:
                new_conv.append(jnp.concatenate([conv_state, u3], axis=1)[:, seq_len:])
            mix_w = w["b_w_pw2"]
        else:
            dk = w["c_w_gate_up"].shape[2]
            dv = w["c_w_o"].shape[1]
            proj = _linear(h, w["c_w_in"], slot, n_out=w["c_w_in"].shape[2], tm=1024, tn=GLA_IN_TN,
                           epilogue=_ep_plain, out_dtype=F32, gain=w["norm_mix"][layer],
                           name="gla_in")
            if st_gla is None:
                s0 = jnp.zeros((batch, GLA_HEADS, dk // GLA_HEADS, dv // GLA_HEADS), F32)
            else:
                s0 = st_gla[slot]
            mix_in, s_new = _gla(proj, w["c_w_gate_up"][slot], w["c_gate_bias"][slot],
                                 w["c_out_norm"][slot], s0, batch=batch, seq_len=seq_len, dk=dk, dv=dv)
            new_gla.append(s_new)
            mix_w = w["c_w_o"]
        h = _linear(mix_in, mix_w, slot, n_out=d, tm=512, tn=d, epilogue=_ep_residual,
                    out_dtype=F32, extras=[_residual_extra(h, 512, d)], name="mixer_out")
        dff = w["ffn_w_down"].shape[1]
        if st_ffn is None:
            ffn_state = jnp.zeros((batch, w["ffn_conv_w"].shape[1] - 1, dff), F32)
        else:
            ffn_state = st_ffn[layer]
        h, ns = _ffn(h, w["norm_ffn"][layer], w["ffn_w_up"], w["ffn_conv_w"][layer],
                     w["ffn_conv_b"][layer], w["ffn_w_down"], layer, ffn_state, seq_len=seq_len)
        new_ffn.append(ns)
        tm = _row_tile(m, 512)
        ple_dim = p.shape[-1]
        h = _linear(h, w["ple_w_gate"], layer, n_out=d, tm=512, tn=d, epilogue=_ep_ple,
                    out_dtype=F32, gain=w["ple_norm"][layer],
                    extras=[(p, (None, tm, ple_dim), lambda i, j, layer=layer: (layer, i, 0)),
                            (w["ple_w_proj"], (None, ple_dim, d),
                             lambda i, j, layer=layer: (layer, 0, j))],
                    name="ple")
    kv_shape = (len(new_k), batch, WINDOW, A_KV_HEADS, A_HEAD_DIM)
    return (h.reshape(batch, seq_len, d),
            jnp.stack(new_k).reshape(kv_shape), jnp.stack(new_v).reshape(kv_shape),
            jnp.stack(new_conv), jnp.stack(new_gla), jnp.stack(new_ffn))


def kernel(x_prompt, x_sample, p_prompt, p_sample, cache_k_a, cache_v_a, state_conv_b, state_gla_c, state_ffn_conv, norm_mix, norm_ffn, a_w_qkv, a_q_norm, a_k_norm, a_sinks, a_w_o, b_w_pw1, b_w_dw, b_dw_bias, b_ln_g, b_ln_b, b_w_pw2, c_w_in, c_w_gate_up, c_gate_bias, c_out_norm, c_w_o, ffn_w_up, ffn_conv_w, ffn_conv_b, ffn_w_down, ple_w_proj, ple_norm, ple_w_gate):
    bp, lp, d = x_prompt.shape
    bs, ls, _ = x_sample.shape
    depth = norm_mix.shape[0]
    gla_pad = (-c_w_in.shape[2]) % GLA_IN_TN
    weights = dict(
        norm_mix=norm_mix, norm_ffn=norm_ffn,
        a_w_qkv=a_w_qkv.astype(BF16), a_q_norm=a_q_norm, a_k_norm=a_k_norm, a_sinks=a_sinks,
        a_w_o=a_w_o.astype(BF16),
        b_w_pw1=b_w_pw1.astype(BF16), b_w_dw=b_w_dw, b_dw_bias=b_dw_bias, b_ln_g=b_ln_g,
        b_ln_b=b_ln_b, b_w_pw2=b_w_pw2.astype(BF16),
        c_w_in=jnp.pad(c_w_in, ((0, 0), (0, 0), (0, gla_pad))).astype(BF16),
        c_w_gate_up=c_w_gate_up, c_gate_bias=c_gate_bias, c_out_norm=c_out_norm,
        c_w_o=c_w_o.astype(BF16),
        ffn_w_up=ffn_w_up.astype(BF16), ffn_conv_w=ffn_conv_w, ffn_conv_b=ffn_conv_b,
        ffn_w_down=ffn_w_down.astype(BF16),
        ple_w_proj=ple_w_proj.astype(BF16), ple_norm=ple_norm, ple_w_gate=ple_w_gate.astype(BF16),
    )
    prompt = _run_group(x_prompt.reshape(bp * lp, d), p_prompt.reshape(depth, bp * lp, -1),
                        batch=bp, seq_len=lp, first_pos=0,
                        states=(None, None, None, None, None), weights=weights)
    sample = _run_group(x_sample.reshape(bs * ls, d), p_sample.reshape(depth, bs * ls, -1),
                        batch=bs, seq_len=ls, first_pos=PAST_LEN,
                        states=(cache_k_a, cache_v_a, state_conv_b, state_gla_c, state_ffn_conv),
                        weights=weights)
    return (prompt[0], sample[0]) + prompt[1:] + sample[1:]
```

```python
import functools

import jax
import jax.numpy as jnp
from jax import lax
from jax.experimental import pallas as pl
from jax.experimental.pallas import tpu as pltpu

F32 = jnp.float32
BF16 = jnp.bfloat16

EPS = 1e-6
CHUNK = 64
WINDOW = 128
A_HEAD_DIM = 64
A_KV_HEADS = 4
ROPE_THETA = 10000.0
PAST_LEN = 2048
GLA_HEADS = 4
GLA_TAU = 16.0
GLA_IN_TN = 1280
LANES = 128
SUBLANES = 8
VMEM_LIMIT_BYTES = 56 * 2**20


def _params(n_axes):
    return pltpu.CompilerParams(dimension_semantics=("arbitrary",) * n_axes,
                                vmem_limit_bytes=VMEM_LIMIT_BYTES)


def _sigmoid(x):
    return 1.0 / (1.0 + jnp.exp(-x))


def _silu(x):
    return x * _sigmoid(x)


def _rms_to_bf16(x, g):
    ms = jnp.mean(x * x, axis=-1, keepdims=True)
    return ((x * lax.rsqrt(ms + EPS)) * g).astype(BF16)


def _row_tile(m, want):
    t = min(m, want)
    assert m % t == 0, (m, t)
    return t


def _linear_body(*refs, has_norm, n_w, n_extra, epilogue):
    it = iter(refs)
    x_ref = next(it)
    g_ref = next(it) if has_norm else None
    w_refs = [next(it) for _ in range(n_w)]
    extra_refs = [next(it) for _ in range(n_extra)]
    o_ref = next(it)
    if has_norm:
        xn_ref = next(it)

        @pl.when(pl.program_id(1) == 0)
        def _():
            xn_ref[...] = _rms_to_bf16(x_ref[...], g_ref[...])

        lhs = xn_ref[...]
    else:
        lhs = x_ref[...].astype(BF16)
    accs = [jnp.dot(lhs, w[...], preferred_element_type=F32) for w in w_refs]
    o_ref[...] = epilogue(accs, extra_refs, x_ref).astype(o_ref.dtype)


def _linear(x, w, layer, *, n_out, tm, tn, epilogue, out_dtype, name, gain=None,
            w_offsets=(0,), extras=()):
    m, k = x.shape
    tm = _row_tile(m, tm)
    assert n_out % tn == 0
    in_specs = [pl.BlockSpec((tm, k), lambda i, j: (i, 0))]
    args = [x]
    if gain is not None:
        in_specs.append(pl.BlockSpec((1, k), lambda i, j: (0, 0)))
        args.append(gain.reshape(1, k))
    w_mode = dict(pipeline_mode=pl.Buffered(1)) if tn == n_out and len(w_offsets) == 1 else {}
    for off in w_offsets:
        in_specs.append(pl.BlockSpec((None, k, tn), lambda i, j, off=off: (layer, 0, j + off), **w_mode))
        args.append(w)
    for arr, shape, imap in extras:
        in_specs.append(pl.BlockSpec(shape, imap))
        args.append(arr)
    body = functools.partial(_linear_body, has_norm=gain is not None, n_w=len(w_offsets),
                             n_extra=len(extras), epilogue=epilogue)
    return pl.pallas_call(
        body,
        grid=(m // tm, n_out // tn),
        in_specs=in_specs,
        out_specs=pl.BlockSpec((tm, tn), lambda i, j: (i, j)),
        out_shape=jax.ShapeDtypeStruct((m, n_out), out_dtype),
        scratch_shapes=[pltpu.VMEM((tm, k), BF16)] if gain is not None else [],
        compiler_params=_params(2),
        name=name,
    )(*args)


def _ep_plain(accs, extras, x_ref):
    return accs[0]


def _ep_residual(accs, extras, x_ref):
    return extras[0][...] + accs[0]


def _ep_glu(accs, extras, x_ref):
    return accs[0] * _sigmoid(accs[1])


def _ep_ple(accs, extras, x_ref):
    p_ref, wp_ref = extras
    proj = jnp.dot(p_ref[...].astype(BF16), wp_ref[...], preferred_element_type=F32)
    return x_ref[...] + proj * _sigmoid(accs[0])


def _residual_extra(h, tm, tn):
    return (h, (_row_tile(h.shape[0], tm), tn), lambda i, j: (i, j))


def _rope_tables(pos):
    half = A_HEAD_DIM // 2
    inv = 1.0 / (ROPE_THETA ** (jnp.arange(half, dtype=F32) / half))
    ang = pos.astype(F32)[:, None] * inv[None, :]
    cos, sin = jnp.cos(ang), jnp.sin(ang)
    return jnp.tile(cos, (1, 4)), jnp.tile(jnp.concatenate([-sin, sin], axis=1), (1, 2))


def _attn_in_body(x_ref, gain_ref, w_ref, cos_ref, sin_ref, qg_ref, kg_ref, q_ref, k_ref, v_ref, *,
                  n_q_pairs, n_k_pairs):
    xn = _rms_to_bf16(x_ref[...], gain_ref[...])
    cos = cos_ref[...]
    sin = sin_ref[...]
    lane = lax.broadcasted_iota(jnp.int32, cos.shape, 1)
    left = lane < A_HEAD_DIM
    first_half = (lane & (A_HEAD_DIM // 2)) == 0
    n_pairs = w_ref.shape[1] // LANES
    for c in range(0, n_pairs, 2):
        acc = jnp.dot(xn, w_ref[:, c * LANES:(c + 2) * LANES], preferred_element_type=F32)
        for p in (c, c + 1):
            x = acc[:, (p - c) * LANES:(p - c + 1) * LANES]
            if p >= n_q_pairs + n_k_pairs:
                pv = p - n_q_pairs - n_k_pairs
                v_ref[:, pv * LANES:(pv + 1) * LANES] = x
                continue
            is_q = p < n_q_pairs
            sq = x * x
            s_left = jnp.sum(jnp.where(left, sq, 0.0), axis=-1, keepdims=True)
            s_right = jnp.sum(jnp.where(left, 0.0, sq), axis=-1, keepdims=True)
            ms = jnp.where(left, s_left, s_right) * (1.0 / A_HEAD_DIM)
            y = (x * lax.rsqrt(ms + EPS)) * (qg_ref[...] if is_q else kg_ref[...])
            partner = jnp.where(first_half, pltpu.roll(y, LANES - A_HEAD_DIM // 2, 1),
                                pltpu.roll(y, A_HEAD_DIM // 2, 1))
            out = y * cos + partner * sin
            if is_q:
                q_ref[p] = (out * (A_HEAD_DIM ** -0.5)).astype(BF16)
            else:
                pk = p - n_q_pairs
                k_ref[:, pk * LANES:(pk + 1) * LANES] = out


def _attn_in(h, gain, w_qkv, slot, cos, sin, q_norm, k_norm, *, seq_len, n_heads):
    m, d = h.shape
    width = w_qkv.shape[2]
    tm = _row_tile(m, 512)
    if seq_len < tm:
        cos = jnp.tile(cos, (tm // seq_len, 1))
        sin = jnp.tile(sin, (tm // seq_len, 1))
    n_tab = cos.shape[0] // tm
    n_q_pairs = n_heads * A_HEAD_DIM // LANES
    dk = A_KV_HEADS * A_HEAD_DIM
    assert width == n_q_pairs * LANES + 2 * dk and (width // LANES) % 2 == 0
    body = functools.partial(_attn_in_body, n_q_pairs=n_q_pairs, n_k_pairs=dk // LANES)
    tab_spec = pl.BlockSpec((tm, LANES), lambda i: (i % n_tab, 0))
    gain_spec = pl.BlockSpec((1, LANES), lambda i: (0, 0))
    kv_spec = pl.BlockSpec((tm, dk), lambda i: (i, 0))
    return pl.pallas_call(
        body,
        grid=(m // tm,),
        in_specs=[pl.BlockSpec((tm, d), lambda i: (i, 0)),
                  pl.BlockSpec((1, d), lambda i: (0, 0)),
                  pl.BlockSpec((None, d, width), lambda i: (slot, 0, 0)),
                  tab_spec, tab_spec, gain_spec, gain_spec],
        out_specs=[pl.BlockSpec((n_q_pairs, tm, LANES), lambda i: (0, i, 0)), kv_spec, kv_spec],
        out_shape=[jax.ShapeDtypeStruct((n_q_pairs, m, LANES), BF16),
                   jax.ShapeDtypeStruct((m, dk), F32), jax.ShapeDtypeStruct((m, dk), F32)],
        compiler_params=_params(1),
        name="attn_in",
    )(h, gain.reshape(1, d), w_qkv, cos, sin, jnp.tile(q_norm, 2).reshape(1, LANES),
      jnp.tile(k_norm, 2).reshape(1, LANES))


def _even_odd_stack(x_all, g):
    pair = x_all[:, (g // 2) * LANES:(g // 2 + 1) * LANES]
    swapped = pltpu.roll(pair, A_HEAD_DIM, 1)
    low = lax.broadcasted_iota(jnp.int32, pair.shape, 1) < A_HEAD_DIM
    if g % 2 == 0:
        lo, hi = jnp.where(low, pair, 0.0), jnp.where(low, 0.0, swapped)
    else:
        lo, hi = jnp.where(low, swapped, 0.0), jnp.where(low, 0.0, pair)
    return jnp.concatenate([lo, hi], axis=0).astype(BF16)


def _attn_groups(q_ref, k_all, v_all, sinks_ref, bias_ref, o_ref, n_heads):
    tq = q_ref.shape[1]
    s_len = k_all.shape[0]
    ppg = n_heads // A_KV_HEADS // 2
    for g in range(A_KV_HEADS):
        k_cat = _even_odd_stack(k_all, g)
        v_cat = _even_odd_stack(v_all, g)
        q = q_ref[g * ppg:(g + 1) * ppg].reshape(ppg * tq, LANES)
        s = lax.dot_general(q, k_cat, (((1,), (1,)), ((), ())), preferred_element_type=F32)
        s = s + bias_ref[...]
        probs = []
        for p in range(ppg):
            halves = []
            for half in range(2):
                sh = s[p * tq:(p + 1) * tq, half * s_len:(half + 1) * s_len]
                sink = sinks_ref[2 * (g * ppg + p) + half]
                mx = jnp.maximum(jnp.max(sh, axis=-1, keepdims=True), sink)
                e = jnp.exp(sh - mx)
                denom = jnp.sum(e, axis=-1, keepdims=True) + jnp.exp(sink - mx)
                halves.append((e * (1.0 / denom)).astype(BF16))
            probs.append(jnp.concatenate(halves, axis=1))
        o = jnp.dot(jnp.concatenate(probs, axis=0), v_cat, preferred_element_type=F32)
        for p in range(ppg):
            col = (g * ppg + p) * LANES
            o_ref[:, col:col + LANES] = o[p * tq:(p + 1) * tq].astype(o_ref.dtype)


def _attn_prompt_body(sinks_ref, bias_ref, q_ref, kp_ref, kc_ref, vp_ref, vc_ref, o_ref, *, n_heads):
    k_all = jnp.concatenate([kp_ref[...], kc_ref[...]], axis=0)
    v_all = jnp.concatenate([vp_ref[...], vc_ref[...]], axis=0)
    _attn_groups(q_ref, k_all, v_all, sinks_ref, bias_ref, o_ref, n_heads)


def _band_bias(tq, rows):
    s_len = WINDOW + tq
    q_chunk = (jnp.arange(rows)[:, None] % tq) // CHUNK
    k_chunk = (jnp.arange(2 * s_len)[None, :] % s_len) // CHUNK - WINDOW // CHUNK
    band = (k_chunk <= q_chunk) & (k_chunk >= q_chunk - WINDOW // CHUNK)
    visible = jnp.stack([band & (k_chunk >= 0), band])
    return jnp.where(visible, 0.0, -jnp.inf).astype(F32)


def _attn_prompt(q, k, v, sinks, *, batch, seq_len, n_heads):
    n_pairs, m, _ = q.shape
    tq = WINDOW
    nq = seq_len // tq
    dk = A_KV_HEADS * A_HEAD_DIM
    d_out = n_heads * A_HEAD_DIM
    rows = n_pairs // A_KV_HEADS * tq
    bias = _band_bias(tq, rows)

    def prev_row(b, i):
        return jnp.maximum(b * nq + i - 1, 0)

    prev_spec = pl.BlockSpec((WINDOW, dk), lambda b, i: (prev_row(b, i), 0))
    cur_spec = pl.BlockSpec((tq, dk), lambda b, i: (b * nq + i, 0))
    return pl.pallas_call(
        functools.partial(_attn_prompt_body, n_heads=n_heads),
        grid=(batch, nq),
        in_specs=[
            pl.BlockSpec(memory_space=pltpu.SMEM),
            pl.BlockSpec((None,) + bias.shape[1:], lambda b, i: (jnp.minimum(i, 1), 0, 0)),
            pl.BlockSpec((n_pairs, tq, LANES), lambda b, i: (0, b * nq + i, 0)),
            prev_spec, cur_spec, prev_spec, cur_spec,
        ],
        out_specs=pl.BlockSpec((tq, d_out), lambda b, i: (b * nq + i, 0)),
        out_shape=jax.ShapeDtypeStruct((m, d_out), BF16),
        compiler_params=_params(2),
        name="attn_prompt",
    )(sinks, bias, q, k, k, v, v)


def _attn_sample_body(sinks_ref, bias_ref, q_ref, kn_ref, vn_ref, ck_ref, cv_ref, o_ref, nk_ref,
                      nv_ref, *, n_heads):
    t = q_ref.shape[1]
    dk = kn_ref.shape[1]
    k_new, v_new = kn_ref[...], vn_ref[...]
    nk_ref[0, 0:WINDOW - t, :] = ck_ref[0, t:WINDOW, :]
    nk_ref[0, WINDOW - t:WINDOW, :] = k_new
    nv_ref[0, 0:WINDOW - t, :] = cv_ref[0, t:WINDOW, :]
    nv_ref[0, WINDOW - t:WINDOW, :] = v_new
    s_len = bias_ref.shape[1] // 2
    zeros = jnp.zeros((s_len - WINDOW - t, dk), F32)
    k_all = jnp.concatenate([ck_ref[0], k_new, zeros], axis=0)
    v_all = jnp.concatenate([cv_ref[0], v_new, zeros], axis=0)
    _attn_groups(q_ref, k_all, v_all, sinks_ref, bias_ref, o_ref, n_heads)


def _attn_sample(q, k, v, cache_k, cache_v, sinks, *, batch, seq_len, n_heads):
    n_pairs, m, _ = q.shape
    dk = A_KV_HEADS * A_HEAD_DIM
    d_out = n_heads * A_HEAD_DIM
    rows = n_pairs // A_KV_HEADS * seq_len
    s_len = -(-(WINDOW + seq_len) // LANES) * LANES
    visible = (jnp.arange(2 * s_len)[None, :] % s_len) < WINDOW + seq_len
    bias = jnp.broadcast_to(jnp.where(visible, 0.0, -jnp.inf).astype(F32), (rows, 2 * s_len))
    cache_spec = pl.BlockSpec((1, WINDOW, dk), lambda b: (b, 0, 0))
    new_spec = pl.BlockSpec((seq_len, dk), lambda b: (b, 0))
    return pl.pallas_call(
        functools.partial(_attn_sample_body, n_heads=n_heads),
        grid=(batch,),
        in_specs=[
            pl.BlockSpec(memory_space=pltpu.SMEM),
            pl.BlockSpec(bias.shape, lambda b: (0, 0)),
            pl.BlockSpec((n_pairs, seq_len, LANES), lambda b: (0, b, 0)),
            new_spec, new_spec, cache_spec, cache_spec,
        ],
        out_specs=[pl.BlockSpec((seq_len, d_out), lambda b: (b, 0)), cache_spec, cache_spec],
        out_shape=[jax.ShapeDtypeStruct((m, d_out), BF16),
                   jax.ShapeDtypeStruct((batch, WINDOW, dk), F32),
                   jax.ShapeDtypeStruct((batch, WINDOW, dk), F32)],
        compiler_params=_params(1),
        name="attn_sample",
    )(sinks, bias, q, k, v, cache_k.reshape(batch, WINDOW, dk), cache_v.reshape(batch, WINDOW, dk))


CONV_HIST = 32
CONV_ROWS = 128
CONV_COLS = 128


def _conv_ln_body(u_ref, st_ref, w_ref, b_ref, g_ref, beta_ref, o_ref, ext_ref, c_ref, *, width):
    tm, d = u_ref.shape
    pad = CONV_HIST - (width - 1)

    @pl.when(pl.program_id(1) == 0)
    def _():
        ext_ref[0:CONV_HIST, :] = st_ref[0]

    ext_ref[CONV_HIST:CONV_HIST + tm, :] = u_ref[...]
    ext_ref[CONV_HIST + tm:CONV_HIST + tm + SUBLANES, :] = jnp.zeros((SUBLANES, d), F32)
    rows = min(CONV_ROWS, tm)
    for cs in range(0, d, CONV_COLS):
        for rb in range(0, tm, rows):
            acc = jnp.broadcast_to(b_ref[:, cs:cs + CONV_COLS], (rows, CONV_COLS))
            for r in range(SUBLANES):
                z = None
                for e in range(r, pad + width, SUBLANES):
                    if e < pad:
                        continue
                    blk = ext_ref[rb + e - r:rb + e - r + rows + SUBLANES, cs:cs + CONV_COLS]
                    term = blk * w_ref[e - pad:e - pad + 1, cs:cs + CONV_COLS]
                    z = term if z is None else z + term
                acc = acc + z[r:r + rows]
            c_ref[rb:rb + rows, cs:cs + CONV_COLS] = acc
    carry = ext_ref[tm:tm + CONV_HIST, :]
    ext_ref[0:CONV_HIST, :] = carry
    for rb in range(0, tm, rows):
        c = c_ref[rb:rb + rows, :]
        mu = jnp.mean(c, axis=-1, keepdims=True)
        xc = c - mu
        var = jnp.mean(xc * xc, axis=-1, keepdims=True)
        y = (xc * lax.rsqrt(var + EPS)) * g_ref[...] + beta_ref[...]
        o_ref[rb:rb + rows, :] = _silu(y).astype(o_ref.dtype)


def _conv_ln(u, state, w_dw, b_dw, ln_g, ln_b, *, batch, seq_len):
    m, d = u.shape
    width = w_dw.shape[0]
    assert width - 1 <= CONV_HIST
    tm = _row_tile(seq_len, 512)
    nt = seq_len // tm
    st = jnp.pad(state, ((0, 0), (CONV_HIST - (width - 1), 0), (0, 0)))
    w = jnp.pad(w_dw, ((0, CONV_HIST - width), (0, 0)))
    row = lambda a: a.reshape(1, d)
    vec_spec = pl.BlockSpec((1, d), lambda b, t: (0, 0))
    return pl.pallas_call(
        functools.partial(_conv_ln_body, width=width),
        grid=(batch, nt),
        in_specs=[pl.BlockSpec((tm, d), lambda b, t: (b * nt + t, 0)),
                  pl.BlockSpec((1, CONV_HIST, d), lambda b, t: (b, 0, 0)),
                  pl.BlockSpec((CONV_HIST, d), lambda b, t: (0, 0)),
                  vec_spec, vec_spec, vec_spec],
        out_specs=pl.BlockSpec((tm, d), lambda b, t: (b * nt + t, 0)),
        out_shape=jax.ShapeDtypeStruct((m, d), BF16),
        scratch_shapes=[pltpu.VMEM((CONV_HIST + tm + SUBLANES, d), F32), pltpu.VMEM((tm, d), F32)],
        compiler_params=_params(2),
        name="conv_ln",
    )(u, st, w, row(b_dw), row(ln_g), row(ln_b))


def _split3(x):
    h1 = x.astype(BF16)
    r1 = x - h1.astype(F32)
    h2 = r1.astype(BF16)
    h3 = (r1 - h2.astype(F32)).astype(BF16)
    return h1, h2, h3


def _exact_dot(dims, a_f32, b_bf16):
    return sum(lax.dot_general(piece, b_bf16, dims, preferred_element_type=F32)
               for piece in _split3(a_f32))


def _gla_body(proj_ref, wgu_ref, bias_ref, norm_ref, s0_ref, o_ref, sout_ref, s_ref, *, dk, dv):
    n_seq, c, _ = proj_ref.shape
    dkh, dvh = dk // GLA_HEADS, dv // GLA_HEADS
    step = pl.program_id(1)

    @pl.when(step == 0)
    def _():
        s_ref[...] = s0_ref[...]

    ri = lax.broadcasted_iota(jnp.int32, (c, c), 0)
    ci = lax.broadcasted_iota(jnp.int32, (c, c), 1)
    causal = ri >= ci
    tri = jnp.where(causal, 1.0, 0.0).astype(BF16)
    last_sel = jnp.where(lax.broadcasted_iota(jnp.int32, (c, LANES), 0) == c - 1, 1.0, 0.0).astype(BF16)
    tn_dims = (((0,), (0,)), ((), ()))
    for g in range(n_seq):
        gl = proj_ref[g, :, 2 * dk + 2 * dv:2 * dk + 2 * dv + LANES].astype(BF16)
        logit = jnp.dot(gl, wgu_ref[...], preferred_element_type=F32) + bias_ref[...]
        log_a = (jnp.minimum(logit, 0.0) - jnp.log1p(jnp.exp(-jnp.abs(logit)))) * (1.0 / GLA_TAU)
        b_all = sum(jnp.dot(tri, piece, preferred_element_type=F32) for piece in _split3(log_a))
        for h in range(GLA_HEADS):
            b = b_all[:, h * dkh:(h + 1) * dkh]
            b_last = b[c - 1:c, :]
            q = proj_ref[g, :, h * dkh:(h + 1) * dkh] * (dkh ** -0.5)
            k = proj_ref[g, :, dk + h * dkh:dk + (h + 1) * dkh]
            v = proj_ref[g, :, 2 * dk + h * dvh:2 * dk + (h + 1) * dvh].astype(BF16)
            r = proj_ref[g, :, 2 * dk + dv + h * dvh:2 * dk + dv + (h + 1) * dvh]
            qt = (q * jnp.exp(b)).astype(BF16)
            kt = (k * jnp.exp(-b)).astype(BF16)
            k_dec = (k * jnp.exp(b_last - b)).astype(BF16)
            s_old = s_ref[g, h]
            o_inter = jnp.dot(qt, s_old.astype(BF16), preferred_element_type=F32)
            a = lax.dot_general(qt, kt, (((1,), (1,)), ((), ())), preferred_element_type=F32)
            a = jnp.where(causal, a, 0.0).astype(BF16)
            o = o_inter + jnp.dot(a, v, preferred_element_type=F32)
            decay = jnp.transpose(jnp.broadcast_to(jnp.exp(b_last), (SUBLANES, dkh)))[:, 0:1]
            s_ref[g, h] = decay * s_old + lax.dot_general(k_dec, v, tn_dims, preferred_element_type=F32)
            ms = jnp.mean(o * o, axis=-1, keepdims=True)
            o_n = (o * lax.rsqrt(ms + EPS)) * norm_ref[...]
            o_ref[g, :, h * dvh:(h + 1) * dvh] = (o_n * _silu(r)).astype(o_ref.dtype)

    @pl.when(step == pl.num_programs(1) - 1)
    def _():
        sout_ref[...] = s_ref[...]


def _gla(proj, wgu, bias, out_norm, state, *, batch, seq_len, dk, dv):
    m, width = proj.shape
    c = min(CHUNK, seq_len)
    nc = seq_len // c
    n_seq = 2 if batch % 2 == 0 else 1
    dkh, dvh = dk // GLA_HEADS, dv // GLA_HEADS
    rank = wgu.shape[0]
    assert rank <= LANES and 2 * dk + 2 * dv + LANES <= width
    wgu_pad = jnp.pad(wgu, ((0, LANES - rank), (0, 0))).astype(BF16)
    state_spec = pl.BlockSpec((n_seq, GLA_HEADS, dkh, dvh), lambda b, t: (b, 0, 0, 0))
    o, s_new = pl.pallas_call(
        functools.partial(_gla_body, dk=dk, dv=dv),
        grid=(batch // n_seq, nc),
        in_specs=[pl.BlockSpec((n_seq, c, width), lambda b, t: (b, t, 0)),
                  pl.BlockSpec(wgu_pad.shape, lambda b, t: (0, 0)),
                  pl.BlockSpec((1, dk), lambda b, t: (0, 0)),
                  pl.BlockSpec((1, dvh), lambda b, t: (0, 0)),
                  state_spec],
        out_specs=[pl.BlockSpec((n_seq, c, dv), lambda b, t: (b, t, 0)), state_spec],
        out_shape=[jax.ShapeDtypeStruct((batch, seq_len, dv), BF16),
                   jax.ShapeDtypeStruct((batch, GLA_HEADS, dkh, dvh), F32)],
        scratch_shapes=[pltpu.VMEM((n_seq, GLA_HEADS, dkh, dvh), F32)],
        compiler_params=_params(2),
        name="gla",
    )(proj.reshape(batch, seq_len, width), wgu_pad, bias.reshape(1, dk), out_norm.reshape(1, dvh),
      state)
    return o.reshape(m, dv), s_new


def _ffn_body(x_ref, gain_ref, wg_ref, wu_ref, cw_ref, cb_ref, wd_ref, st_ref, o_ref, ns_ref,
              xn_ref, *carry, seq_tile, tiles_per_seq):
    tm = x_ref.shape[0]
    tn = wg_ref.shape[1]
    nseq = tm // seq_tile
    i = pl.program_id(0)
    j = pl.program_id(1)

    @pl.when(j == 0)
    def _():
        x = x_ref[...]
        xn_ref[...] = _rms_to_bf16(x, gain_ref[...])
        o_ref[...] = x

    xn = xn_ref[...]
    g = jnp.dot(xn, wg_ref[...], preferred_element_type=F32)
    u = jnp.dot(xn, wu_ref[...], preferred_element_type=F32)
    if tiles_per_seq > 1:
        prev = jnp.where(i % tiles_per_seq == 0, st_ref[...], carry[0][j])
    else:
        prev = st_ref[...]
    g3 = g.reshape(nseq, seq_tile, tn)
    row = lax.broadcasted_iota(jnp.int32, (nseq, seq_tile, tn), 1)
    back1 = jnp.where(row == 0, prev[:, 1:2, :], pltpu.roll(g, 1, 0).reshape(nseq, seq_tile, tn))
    back2 = jnp.where(row == 0, prev[:, 0:1, :],
                      jnp.where(row == 1, prev[:, 1:2, :],
                                pltpu.roll(g, 2, 0).reshape(nseq, seq_tile, tn)))
    gc = (back2 * cw_ref[0:1, :] + back1 * cw_ref[1:2, :] + g3 * cw_ref[2:3, :]) + cb_ref[...]
    act = (_silu(gc).reshape(tm, tn) * u).astype(BF16)
    o_ref[...] += jnp.dot(act, wd_ref[...], preferred_element_type=F32)
    tail = g3[:, seq_tile - 2:seq_tile, :]
    ns_ref[...] = tail
    if tiles_per_seq > 1:
        carry[0][j] = tail


def _ffn(h, gain, w_up, conv_w, conv_b, w_down, layer, state, *, seq_len, tn=512):
    m, d = h.shape
    dff = w_down.shape[1]
    assert dff % tn == 0
    nj = dff // tn
    tm = _row_tile(m, 1024)
    seq_tile = min(seq_len, tm)
    tiles_per_seq = seq_len // seq_tile
    nseq = tm // seq_tile
    width = conv_w.shape[0]
    assert width == 3 and state.shape[1] == width - 1
    st_map = lambda i, j: (i // tiles_per_seq, 0, j)
    tail_map = lambda i, j: (i, 0, j)
    scratch = [pltpu.VMEM((tm, d), BF16)]
    if tiles_per_seq > 1:
        scratch.append(pltpu.VMEM((nj, nseq, width - 1, tn), F32))
    out, tails = pl.pallas_call(
        functools.partial(_ffn_body, seq_tile=seq_tile, tiles_per_seq=tiles_per_seq),
        grid=(m // tm, nj),
        in_specs=[pl.BlockSpec((tm, d), lambda i, j: (i, 0)),
                  pl.BlockSpec((1, d), lambda i, j: (0, 0)),
                  pl.BlockSpec((None, d, tn), lambda i, j: (layer, 0, j)),
                  pl.BlockSpec((None, d, tn), lambda i, j: (layer, 0, j + nj)),
                  pl.BlockSpec((width, tn), lambda i, j: (0, j)),
                  pl.BlockSpec((1, tn), lambda i, j: (0, j)),
                  pl.BlockSpec((None, tn, d), lambda i, j: (layer, j, 0)),
                  pl.BlockSpec((nseq, width - 1, tn), st_map)],
        out_specs=[pl.BlockSpec((tm, d), lambda i, j: (i, 0)),
                   pl.BlockSpec((nseq, width - 1, tn), tail_map)],
        out_shape=[jax.ShapeDtypeStruct((m, d), F32),
                   jax.ShapeDtypeStruct((m // seq_tile, width - 1, dff), F32)],
        scratch_shapes=scratch,
        compiler_params=_params(2),
        name="conv_ffn",
    )(h, gain.reshape(1, d), w_up, w_up, conv_w, conv_b.reshape(1, dff), w_down, state)
    return out, tails.reshape(-1, tiles_per_seq, width - 1, dff)[:, -1]


def _run_group(h, p, *, batch, seq_len, first_pos, states, weights):
    (cache_k, cache_v, st_conv, st_gla, st_ffn) = states
    w = weights
    m, d = h.shape
    depth = w["norm_mix"].shape[0]
    n_heads = w["a_w_o"].shape[1] // A_HEAD_DIM
    dkv = A_KV_HEADS * A_HEAD_DIM
    cos, sin = _rope_tables(first_pos + jnp.arange(seq_len))
    new_k, new_v, new_conv, new_gla, new_ffn = [], [], [], [], []
    for layer in range(depth):
        kind, slot = layer % 3, layer // 3
        if kind == 0:
            q, k, v = _attn_in(h, w["norm_mix"][layer], w["a_w_qkv"], slot, cos, sin,
                               w["a_q_norm"][slot], w["a_k_norm"][slot], seq_len=seq_len,
                               n_heads=n_heads)
            if cache_k is None:
                o = _attn_prompt(q, k, v, w["a_sinks"][slot], batch=batch, seq_len=seq_len,
                                 n_heads=n_heads)
                new_k.append(k.reshape(batch, seq_len, dkv)[:, seq_len - WINDOW:])
                new_v.append(v.reshape(batch, seq_len, dkv)[:, seq_len - WINDOW:])
            else:
                o, nk, nv = _attn_sample(q, k, v, cache_k[slot], cache_v[slot], w["a_sinks"][slot],
                                         batch=batch, seq_len=seq_len, n_heads=n_heads)
                new_k.append(nk)
                new_v.append(nv)
            mix_in, mix_w = o, w["a_w_o"]
        elif kind == 1:
            tn = 1024
            u = _linear(h, w["b_w_pw1"], slot, n_out=d, tm=1024, tn=tn, epilogue=_ep_glu,
                        out_dtype=F32, gain=w["norm_mix"][layer], w_offsets=(0, d // tn),
                        name="conf_pw1_glu")
            hist = w["b_w_dw"].shape[1] - 1
            if st_conv is None:
                conv_state = jnp.zeros((batch, hist, d), F32)
            else:
                conv_state = st_conv[slot]
            mix_in = _conv_ln(u, conv_state, w["b_w_dw"][slot], w["b_dw_bias"][slot],
                              w["b_ln_g"][slot], w["b_ln_b"][slot], batch=batch, seq_len=seq_len)
            u3 = u.reshape(batch, seq_len, d)
            if seq_len >= hist:
                new_conv.append(u3[:, seq_len - hist:])
            else:
                new_conv.append(jnp.concatenate([conv_state, u3], axis=1)[:, seq_len:])
            mix_w = w["b_w_pw2"]
        else:
            dk = w["c_w_gate_up"].shape[2]
            dv = w["c_w_o"].shape[1]
            proj = _linear(h, w["c_w_in"], slot, n_out=w["c_w_in"].shape[2], tm=1024, tn=GLA_IN_TN,
                           epilogue=_ep_plain, out_dtype=F32, gain=w["norm_mix"][layer],
                           name="gla_in")
            if st_gla is None:
                s0 = jnp.zeros((batch, GLA_HEADS, dk // GLA_HEADS, dv // GLA_HEADS), F32)
            else:
                s0 = st_gla[slot]
            mix_in, s_new = _gla(proj, w["c_w_gate_up"][slot], w["c_gate_bias"][slot],
                                 w["c_out_norm"][slot], s0, batch=batch, seq_len=seq_len, dk=dk, dv=dv)
            new_gla.append(s_new)
            mix_w = w["c_w_o"]
        h = _linear(mix_in, mix_w, slot, n_out=d, tm=1024, tn=d, epilogue=_ep_residual,
                    out_dtype=F32, extras=[_residual_extra(h, 1024, d)], name="mixer_out")
        dff = w["ffn_w_down"].shape[1]
        if st_ffn is None:
            ffn_state = jnp.zeros((batch, w["ffn_conv_w"].shape[1] - 1, dff), F32)
        else:
            ffn_state = st_ffn[layer]
        h, ns = _ffn(h, w["norm_ffn"][layer], w["ffn_w_up"], w["ffn_conv_w"][layer],
                     w["ffn_conv_b"][layer], w["ffn_w_down"], layer, ffn_state, seq_len=seq_len)
        new_ffn.append(ns)
        tm = _row_tile(m, 1024)
        ple_dim = p.shape[-1]
        h = _linear(h, w["ple_w_gate"], layer, n_out=d, tm=1024, tn=d, epilogue=_ep_ple,
                    out_dtype=F32, gain=w["ple_norm"][layer],
                    extras=[(p, (None, tm, ple_dim), lambda i, j, layer=layer: (layer, i, 0)),
                            (w["ple_w_proj"], (None, ple_dim, d),
                             lambda i, j, layer=layer: (layer, 0, j))],
                    name="ple")
    kv_shape = (len(new_k), batch, WINDOW, A_KV_HEADS, A_HEAD_DIM)
    return (h.reshape(batch, seq_len, d),
            jnp.stack(new_k).reshape(kv_shape), jnp.stack(new_v).reshape(kv_shape),
            jnp.stack(new_conv), jnp.stack(new_gla), jnp.stack(new_ffn))


def kernel(x_prompt, x_sample, p_prompt, p_sample, cache_k_a, cache_v_a, state_conv_b, state_gla_c, state_ffn_conv, norm_mix, norm_ffn, a_w_qkv, a_q_norm, a_k_norm, a_sinks, a_w_o, b_w_pw1, b_w_dw, b_dw_bias, b_ln_g, b_ln_b, b_w_pw2, c_w_in, c_w_gate_up, c_gate_bias, c_out_norm, c_w_o, ffn_w_up, ffn_conv_w, ffn_conv_b, ffn_w_down, ple_w_proj, ple_norm, ple_w_gate):
    bp, lp, d = x_prompt.shape
    bs, ls, _ = x_sample.shape
    depth = norm_mix.shape[0]
    gla_pad = (-c_w_in.shape[2]) % GLA_IN_TN
    weights = dict(
        norm_mix=norm_mix, norm_ffn=norm_ffn,
        a_w_qkv=a_w_qkv.astype(BF16), a_q_norm=a_q_norm, a_k_norm=a_k_norm, a_sinks=a_sinks,
        a_w_o=a_w_o.astype(BF16),
        b_w_pw1=b_w_pw1.astype(BF16), b_w_dw=b_w_dw, b_dw_bias=b_dw_bias, b_ln_g=b_ln_g,
        b_ln_b=b_ln_b, b_w_pw2=b_w_pw2.astype(BF16),
        c_w_in=jnp.pad(c_w_in, ((0, 0), (0, 0), (0, gla_pad))).astype(BF16),
        c_w_gate_up=c_w_gate_up, c_gate_bias=c_gate_bias, c_out_norm=c_out_norm,
        c_w_o=c_w_o.astype(BF16),
        ffn_w_up=ffn_w_up.astype(BF16), ffn_conv_w=ffn_conv_w, ffn_conv_b=ffn_conv_b,
        ffn_w_down=ffn_w_down.astype(BF16),
        ple_w_proj=ple_w_proj.astype(BF16), ple_norm=ple_norm, ple_w_gate=ple_w_gate.astype(BF16),
    )
    prompt = _run_group(x_prompt.reshape(bp * lp, d), p_prompt.reshape(depth, bp * lp, -1),
                        batch=bp, seq_len=lp, first_pos=0,
                        states=(None, None, None, None, None), weights=weights)
    sample = _run_group(x_sample.reshape(bs * ls, d), p_sample.reshape(depth, bs * ls, -1),
                        batch=bs, seq_len=ls, first_pos=PAST_LEN,
                        states=(cache_k_a, cache_v_a, state_conv_b, state_gla_c, state_ffn_conv),
                        weights=weights)
    return (prompt[0], sample[0]) + prompt[1:] + sample[1:]
```

```python
import functools

import jax
import jax.numpy as jnp
from jax import lax
from jax.experimental import pallas as pl
from jax.experimental.pallas import tpu as pltpu

F32 = jnp.float32
BF16 = jnp.bfloat16

EPS = 1e-6
CHUNK = 64
WINDOW = 128
A_HEAD_DIM = 64
A_KV_HEADS = 4
ROPE_THETA = 10000.0
PAST_LEN = 2048
GLA_HEADS = 4
GLA_TAU = 16.0
GLA_IN_TN = 1280
LANES = 128
SUBLANES = 8
VMEM_LIMIT_BYTES = 56 * 2**20


def _params(n_axes):
    return pltpu.CompilerParams(dimension_semantics=("arbitrary",) * n_axes,
                                vmem_limit_bytes=VMEM_LIMIT_BYTES)


def _sigmoid(x):
    return 1.0 / (1.0 + jnp.exp(-x))


def _silu(x):
    return x * _sigmoid(x)


def _rms_to_bf16(x, g):
    ms = jnp.mean(x * x, axis=-1, keepdims=True)
    return ((x * lax.rsqrt(ms + EPS)) * g).astype(BF16)


def _row_tile(m, want):
    t = min(m, want)
    assert m % t == 0, (m, t)
    return t


def _linear_body(*refs, has_norm, n_w, n_extra, epilogue):
    it = iter(refs)
    x_ref = next(it)
    g_ref = next(it) if has_norm else None
    w_refs = [next(it) for _ in range(n_w)]
    extra_refs = [next(it) for _ in range(n_extra)]
    o_ref = next(it)
    if has_norm:
        xn_ref = next(it)

        @pl.when(pl.program_id(1) == 0)
        def _():
            xn_ref[...] = _rms_to_bf16(x_ref[...], g_ref[...])

        lhs = xn_ref[...]
    else:
        lhs = x_ref[...].astype(BF16)
    accs = [jnp.dot(lhs, w[...], preferred_element_type=F32) for w in w_refs]
    o_ref[...] = epilogue(accs, extra_refs, x_ref).astype(o_ref.dtype)


def _linear(x, w, layer, *, n_out, tm, tn, epilogue, out_dtype, name, gain=None,
            w_offsets=(0,), extras=()):
    m, k = x.shape
    tm = _row_tile(m, tm)
    assert n_out % tn == 0
    in_specs = [pl.BlockSpec((tm, k), lambda i, j: (i, 0))]
    args = [x]
    if gain is not None:
        in_specs.append(pl.BlockSpec((1, k), lambda i, j: (0, 0)))
        args.append(gain.reshape(1, k))
    for off in w_offsets:
        in_specs.append(pl.BlockSpec((None, k, tn), lambda i, j, off=off: (layer, 0, j + off)))
        args.append(w)
    for arr, shape, imap in extras:
        in_specs.append(pl.BlockSpec(shape, imap))
        args.append(arr)
    body = functools.partial(_linear_body, has_norm=gain is not None, n_w=len(w_offsets),
                             n_extra=len(extras), epilogue=epilogue)
    return pl.pallas_call(
        body,
        grid=(m // tm, n_out // tn),
        in_specs=in_specs,
        out_specs=pl.BlockSpec((tm, tn), lambda i, j: (i, j)),
        out_shape=jax.ShapeDtypeStruct((m, n_out), out_dtype),
        scratch_shapes=[pltpu.VMEM((tm, k), BF16)] if gain is not None else [],
        compiler_params=_params(2),
        name=name,
    )(*args)


def _ep_plain(accs, extras, x_ref):
    return accs[0]


def _ep_residual(accs, extras, x_ref):
    return extras[0][...] + accs[0]


def _ep_glu(accs, extras, x_ref):
    return accs[0] * _sigmoid(accs[1])


def _ep_ple(accs, extras, x_ref):
    p_ref, wp_ref = extras
    proj = jnp.dot(p_ref[...].astype(BF16), wp_ref[...], preferred_element_type=F32)
    return x_ref[...] + proj * _sigmoid(accs[0])


def _residual_extra(h, tm, tn):
    return (h, (_row_tile(h.shape[0], tm), tn), lambda i, j: (i, j))


def _rope_tables(pos):
    half = A_HEAD_DIM // 2
    inv = 1.0 / (ROPE_THETA ** (jnp.arange(half, dtype=F32) / half))
    ang = pos.astype(F32)[:, None] * inv[None, :]
    cos, sin = jnp.cos(ang), jnp.sin(ang)
    return jnp.tile(cos, (1, 4)), jnp.tile(jnp.concatenate([-sin, sin], axis=1), (1, 2))


def _attn_in_body(x_ref, gain_ref, w_ref, cos_ref, sin_ref, qg_ref, kg_ref, q_ref, k_ref, v_ref, *,
                  n_q_pairs, n_k_pairs):
    xn = _rms_to_bf16(x_ref[...], gain_ref[...])
    cos = cos_ref[...]
    sin = sin_ref[...]
    lane = lax.broadcasted_iota(jnp.int32, cos.shape, 1)
    left = lane < A_HEAD_DIM
    first_half = (lane & (A_HEAD_DIM // 2)) == 0
    n_pairs = w_ref.shape[1] // LANES
    for c in range(0, n_pairs, 2):
        acc = jnp.dot(xn, w_ref[:, c * LANES:(c + 2) * LANES], preferred_element_type=F32)
        for p in (c, c + 1):
            x = acc[:, (p - c) * LANES:(p - c + 1) * LANES]
            if p >= n_q_pairs + n_k_pairs:
                pv = p - n_q_pairs - n_k_pairs
                v_ref[:, pv * LANES:(pv + 1) * LANES] = x
                continue
            is_q = p < n_q_pairs
            sq = x * x
            s_left = jnp.sum(jnp.where(left, sq, 0.0), axis=-1, keepdims=True)
            s_right = jnp.sum(jnp.where(left, 0.0, sq), axis=-1, keepdims=True)
            ms = jnp.where(left, s_left, s_right) * (1.0 / A_HEAD_DIM)
            y = (x * lax.rsqrt(ms + EPS)) * (qg_ref[...] if is_q else kg_ref[...])
            partner = jnp.where(first_half, pltpu.roll(y, LANES - A_HEAD_DIM // 2, 1),
                                pltpu.roll(y, A_HEAD_DIM // 2, 1))
            out = y * cos + partner * sin
            if is_q:
                q_ref[p] = (out * (A_HEAD_DIM ** -0.5)).astype(BF16)
            else:
                pk = p - n_q_pairs
                k_ref[:, pk * LANES:(pk + 1) * LANES] = out


def _attn_in(h, gain, w_qkv, slot, cos, sin, q_norm, k_norm, *, seq_len, n_heads):
    m, d = h.shape
    width = w_qkv.shape[2]
    tm = _row_tile(m, 512)
    if seq_len < tm:
        cos = jnp.tile(cos, (tm // seq_len, 1))
        sin = jnp.tile(sin, (tm // seq_len, 1))
    n_tab = cos.shape[0] // tm
    n_q_pairs = n_heads * A_HEAD_DIM // LANES
    dk = A_KV_HEADS * A_HEAD_DIM
    assert width == n_q_pairs * LANES + 2 * dk and (width // LANES) % 2 == 0
    body = functools.partial(_attn_in_body, n_q_pairs=n_q_pairs, n_k_pairs=dk // LANES)
    tab_spec = pl.BlockSpec((tm, LANES), lambda i: (i % n_tab, 0))
    gain_spec = pl.BlockSpec((1, LANES), lambda i: (0, 0))
    kv_spec = pl.BlockSpec((tm, dk), lambda i: (i, 0))
    return pl.pallas_call(
        body,
        grid=(m // tm,),
        in_specs=[pl.BlockSpec((tm, d), lambda i: (i, 0)),
                  pl.BlockSpec((1, d), lambda i: (0, 0)),
                  pl.BlockSpec((None, d, width), lambda i: (slot, 0, 0)),
                  tab_spec, tab_spec, gain_spec, gain_spec],
        out_specs=[pl.BlockSpec((n_q_pairs, tm, LANES), lambda i: (0, i, 0)), kv_spec, kv_spec],
        out_shape=[jax.ShapeDtypeStruct((n_q_pairs, m, LANES), BF16),
                   jax.ShapeDtypeStruct((m, dk), F32), jax.ShapeDtypeStruct((m, dk), F32)],
        compiler_params=_params(1),
        name="attn_in",
    )(h, gain.reshape(1, d), w_qkv, cos, sin, jnp.tile(q_norm, 2).reshape(1, LANES),
      jnp.tile(k_norm, 2).reshape(1, LANES))


def _even_odd_stack(x_all, g):
    pair = x_all[:, (g // 2) * LANES:(g // 2 + 1) * LANES]
    swapped = pltpu.roll(pair, A_HEAD_DIM, 1)
    low = lax.broadcasted_iota(jnp.int32, pair.shape, 1) < A_HEAD_DIM
    if g % 2 == 0:
        lo, hi = jnp.where(low, pair, 0.0), jnp.where(low, 0.0, swapped)
    else:
        lo, hi = jnp.where(low, swapped, 0.0), jnp.where(low, 0.0, pair)
    return jnp.concatenate([lo, hi], axis=0).astype(BF16)


def _attn_groups(q_ref, k_all, v_all, sinks_ref, bias_ref, o_ref, n_heads):
    tq = q_ref.shape[1]
    s_len = k_all.shape[0]
    ppg = n_heads // A_KV_HEADS // 2
    for g in range(A_KV_HEADS):
        k_cat = _even_odd_stack(k_all, g)
        v_cat = _even_odd_stack(v_all, g)
        q = q_ref[g * ppg:(g + 1) * ppg].reshape(ppg * tq, LANES)
        s = lax.dot_general(k_cat, q, (((1,), (1,)), ((), ())), preferred_element_type=F32)
        s = s + bias_ref[...]
        lane_pair = lax.broadcasted_iota(jnp.int32, (1, ppg * tq), 1) // tq
        halves = []
        for half in range(2):
            sh = s[half * s_len:(half + 1) * s_len, :]
            sink = jnp.zeros((1, ppg * tq), F32)
            for p in range(ppg):
                sink = jnp.where(lane_pair == p, sinks_ref[2 * (g * ppg + p) + half], sink)
            mx = jnp.maximum(jnp.max(sh, axis=0, keepdims=True), sink)
            e = jnp.exp(sh - mx)
            denom = jnp.sum(e, axis=0, keepdims=True) + jnp.exp(sink - mx)
            halves.append((e * (1.0 / denom)).astype(BF16))
        o = lax.dot_general(jnp.concatenate(halves, axis=0), v_cat, (((0,), (0,)), ((), ())),
                            preferred_element_type=F32)
        for p in range(ppg):
            col = (g * ppg + p) * LANES
            o_ref[:, col:col + LANES] = o[p * tq:(p + 1) * tq].astype(o_ref.dtype)


def _attn_prompt_body(sinks_ref, bias_ref, q_ref, kp_ref, kc_ref, vp_ref, vc_ref, o_ref, *, n_heads):
    k_all = jnp.concatenate([kp_ref[...], kc_ref[...]], axis=0)
    v_all = jnp.concatenate([vp_ref[...], vc_ref[...]], axis=0)
    _attn_groups(q_ref, k_all, v_all, sinks_ref, bias_ref, o_ref, n_heads)


def _band_bias(tq, rows):
    s_len = WINDOW + tq
    q_chunk = (jnp.arange(rows)[:, None] % tq) // CHUNK
    k_chunk = (jnp.arange(2 * s_len)[None, :] % s_len) // CHUNK - WINDOW // CHUNK
    band = (k_chunk <= q_chunk) & (k_chunk >= q_chunk - WINDOW // CHUNK)
    visible = jnp.stack([band & (k_chunk >= 0), band])
    return jnp.where(visible, 0.0, -jnp.inf).astype(F32)


def _attn_prompt(q, k, v, sinks, *, batch, seq_len, n_heads):
    n_pairs, m, _ = q.shape
    tq = WINDOW
    nq = seq_len // tq
    dk = A_KV_HEADS * A_HEAD_DIM
    d_out = n_heads * A_HEAD_DIM
    rows = n_pairs // A_KV_HEADS * tq
    bias = _band_bias(tq, rows).transpose(0, 2, 1)

    def prev_row(b, i):
        return jnp.maximum(b * nq + i - 1, 0)

    prev_spec = pl.BlockSpec((WINDOW, dk), lambda b, i: (prev_row(b, i), 0))
    cur_spec = pl.BlockSpec((tq, dk), lambda b, i: (b * nq + i, 0))
    return pl.pallas_call(
        functools.partial(_attn_prompt_body, n_heads=n_heads),
        grid=(batch, nq),
        in_specs=[
            pl.BlockSpec(memory_space=pltpu.SMEM),
            pl.BlockSpec((None,) + bias.shape[1:], lambda b, i: (jnp.minimum(i, 1), 0, 0)),
            pl.BlockSpec((n_pairs, tq, LANES), lambda b, i: (0, b * nq + i, 0)),
            prev_spec, cur_spec, prev_spec, cur_spec,
        ],
        out_specs=pl.BlockSpec((tq, d_out), lambda b, i: (b * nq + i, 0)),
        out_shape=jax.ShapeDtypeStruct((m, d_out), BF16),
        compiler_params=_params(2),
        name="attn_prompt",
    )(sinks, bias, q, k, k, v, v)


def _attn_sample_body(sinks_ref, bias_ref, q_ref, kn_ref, vn_ref, ck_ref, cv_ref, o_ref, nk_ref,
                      nv_ref, *, n_heads):
    t = q_ref.shape[1]
    dk = kn_ref.shape[1]
    k_new, v_new = kn_ref[...], vn_ref[...]
    nk_ref[0, 0:WINDOW - t, :] = ck_ref[0, t:WINDOW, :]
    nk_ref[0, WINDOW - t:WINDOW, :] = k_new
    nv_ref[0, 0:WINDOW - t, :] = cv_ref[0, t:WINDOW, :]
    nv_ref[0, WINDOW - t:WINDOW, :] = v_new
    s_len = bias_ref.shape[0] // 2
    zeros = jnp.zeros((s_len - WINDOW - t, dk), F32)
    k_all = jnp.concatenate([ck_ref[0], k_new, zeros], axis=0)
    v_all = jnp.concatenate([cv_ref[0], v_new, zeros], axis=0)
    _attn_groups(q_ref, k_all, v_all, sinks_ref, bias_ref, o_ref, n_heads)


def _attn_sample(q, k, v, cache_k, cache_v, sinks, *, batch, seq_len, n_heads):
    n_pairs, m, _ = q.shape
    dk = A_KV_HEADS * A_HEAD_DIM
    d_out = n_heads * A_HEAD_DIM
    rows = n_pairs // A_KV_HEADS * seq_len
    s_len = -(-(WINDOW + seq_len) // LANES) * LANES
    visible = (jnp.arange(2 * s_len)[None, :] % s_len) < WINDOW + seq_len
    bias = jnp.broadcast_to(jnp.where(visible, 0.0, -jnp.inf).astype(F32), (rows, 2 * s_len)).T
    cache_spec = pl.BlockSpec((1, WINDOW, dk), lambda b: (b, 0, 0))
    new_spec = pl.BlockSpec((seq_len, dk), lambda b: (b, 0))
    return pl.pallas_call(
        functools.partial(_attn_sample_body, n_heads=n_heads),
        grid=(batch,),
        in_specs=[
            pl.BlockSpec(memory_space=pltpu.SMEM),
            pl.BlockSpec(bias.shape, lambda b: (0, 0)),
            pl.BlockSpec((n_pairs, seq_len, LANES), lambda b: (0, b, 0)),
            new_spec, new_spec, cache_spec, cache_spec,
        ],
        out_specs=[pl.BlockSpec((seq_len, d_out), lambda b: (b, 0)), cache_spec, cache_spec],
        out_shape=[jax.ShapeDtypeStruct((m, d_out), BF16),
                   jax.ShapeDtypeStruct((batch, WINDOW, dk), F32),
                   jax.ShapeDtypeStruct((batch, WINDOW, dk), F32)],
        compiler_params=_params(1),
        name="attn_sample",
    )(sinks, bias, q, k, v, cache_k.reshape(batch, WINDOW, dk), cache_v.reshape(batch, WINDOW, dk))


CONV_HIST = 32
CONV_ROWS = 128
CONV_COLS = 128


def _conv_ln_body(u_ref, st_ref, w_ref, b_ref, g_ref, beta_ref, o_ref, ext_ref, c_ref, *, width):
    tm, d = u_ref.shape
    pad = CONV_HIST - (width - 1)

    @pl.when(pl.program_id(1) == 0)
    def _():
        ext_ref[0:CONV_HIST, :] = st_ref[0]

    ext_ref[CONV_HIST:CONV_HIST + tm, :] = u_ref[...]
    ext_ref[CONV_HIST + tm:CONV_HIST + tm + SUBLANES, :] = jnp.zeros((SUBLANES, d), F32)
    rows = min(CONV_ROWS, tm)
    for cs in range(0, d, CONV_COLS):
        for rb in range(0, tm, rows):
            acc = jnp.broadcast_to(b_ref[:, cs:cs + CONV_COLS], (rows, CONV_COLS))
            for r in range(SUBLANES):
                z = None
                for e in range(r, pad + width, SUBLANES):
                    if e < pad:
                        continue
                    blk = ext_ref[rb + e - r:rb + e - r + rows + SUBLANES, cs:cs + CONV_COLS]
                    term = blk * w_ref[e - pad:e - pad + 1, cs:cs + CONV_COLS]
                    z = term if z is None else z + term
                acc = acc + z[r:r + rows]
            c_ref[rb:rb + rows, cs:cs + CONV_COLS] = acc
    carry = ext_ref[tm:tm + CONV_HIST, :]
    ext_ref[0:CONV_HIST, :] = carry
    for rb in range(0, tm, rows):
        c = c_ref[rb:rb + rows, :]
        mu = jnp.mean(c, axis=-1, keepdims=True)
        xc = c - mu
        var = jnp.mean(xc * xc, axis=-1, keepdims=True)
        y = (xc * lax.rsqrt(var + EPS)) * g_ref[...] + beta_ref[...]
        o_ref[rb:rb + rows, :] = _silu(y).astype(o_ref.dtype)


def _conv_ln(u, state, w_dw, b_dw, ln_g, ln_b, *, batch, seq_len):
    m, d = u.shape
    width = w_dw.shape[0]
    assert width - 1 <= CONV_HIST
    tm = _row_tile(seq_len, 256)
    nt = seq_len // tm
    st = jnp.pad(state, ((0, 0), (CONV_HIST - (width - 1), 0), (0, 0)))
    w = jnp.pad(w_dw, ((0, CONV_HIST - width), (0, 0)))
    row = lambda a: a.reshape(1, d)
    vec_spec = pl.BlockSpec((1, d), lambda b, t: (0, 0))
    return pl.pallas_call(
        functools.partial(_conv_ln_body, width=width),
        grid=(batch, nt),
        in_specs=[pl.BlockSpec((tm, d), lambda b, t: (b * nt + t, 0)),
                  pl.BlockSpec((1, CONV_HIST, d), lambda b, t: (b, 0, 0)),
                  pl.BlockSpec((CONV_HIST, d), lambda b, t: (0, 0)),
                  vec_spec, vec_spec, vec_spec],
        out_specs=pl.BlockSpec((tm, d), lambda b, t: (b * nt + t, 0)),
        out_shape=jax.ShapeDtypeStruct((m, d), BF16),
        scratch_shapes=[pltpu.VMEM((CONV_HIST + tm + SUBLANES, d), F32), pltpu.VMEM((tm, d), F32)],
        compiler_params=_params(2),
        name="conv_ln",
    )(u, st, w, row(b_dw), row(ln_g), row(ln_b))


def _split3(x):
    h1 = x.astype(BF16)
    r1 = x - h1.astype(F32)
    h2 = r1.astype(BF16)
    h3 = (r1 - h2.astype(F32)).astype(BF16)
    return h1, h2, h3


def _exact_dot(dims, a_f32, b_bf16):
    return sum(lax.dot_general(piece, b_bf16, dims, preferred_element_type=F32)
               for piece in _split3(a_f32))


def _gla_body(proj_ref, wgu_ref, bias_ref, norm_ref, s0_ref, o_ref, sout_ref, s_ref, *, dk, dv):
    n_seq, c, _ = proj_ref.shape
    dkh, dvh = dk // GLA_HEADS, dv // GLA_HEADS
    step = pl.program_id(1)

    @pl.when(step == 0)
    def _():
        s_ref[...] = s0_ref[...]

    ri = lax.broadcasted_iota(jnp.int32, (c, c), 0)
    ci = lax.broadcasted_iota(jnp.int32, (c, c), 1)
    causal = ri >= ci
    tri = jnp.where(causal, 1.0, 0.0).astype(BF16)
    last_sel = jnp.where(lax.broadcasted_iota(jnp.int32, (c, LANES), 0) == c - 1, 1.0, 0.0).astype(BF16)
    tn_dims = (((0,), (0,)), ((), ()))
    for g in range(n_seq):
        gl = proj_ref[g, :, 2 * dk + 2 * dv:2 * dk + 2 * dv + LANES].astype(BF16)
        logit = jnp.dot(gl, wgu_ref[...], preferred_element_type=F32) + bias_ref[...]
        log_a = (jnp.minimum(logit, 0.0) - jnp.log1p(jnp.exp(-jnp.abs(logit)))) * (1.0 / GLA_TAU)
        b_all = sum(jnp.dot(tri, piece, preferred_element_type=F32) for piece in _split3(log_a))
        for h in range(GLA_HEADS):
            b = b_all[:, h * dkh:(h + 1) * dkh]
            b_last = b[c - 1:c, :]
            q = proj_ref[g, :, h * dkh:(h + 1) * dkh] * (dkh ** -0.5)
            k = proj_ref[g, :, dk + h * dkh:dk + (h + 1) * dkh]
            v = proj_ref[g, :, 2 * dk + h * dvh:2 * dk + (h + 1) * dvh].astype(BF16)
            r = proj_ref[g, :, 2 * dk + dv + h * dvh:2 * dk + dv + (h + 1) * dvh]
            qt = (q * jnp.exp(b)).astype(BF16)
            kt = (k * jnp.exp(-b)).astype(BF16)
            k_dec = (k * jnp.exp(b_last - b)).astype(BF16)
            s_old = s_ref[g, h]
            o_inter = jnp.dot(qt, s_old.astype(BF16), preferred_element_type=F32)
            a = lax.dot_general(qt, kt, (((1,), (1,)), ((), ())), preferred_element_type=F32)
            a = jnp.where(causal, a, 0.0).astype(BF16)
            o = o_inter + jnp.dot(a, v, preferred_element_type=F32)
            decay = jnp.transpose(jnp.broadcast_to(jnp.exp(b_last), (SUBLANES, dkh)))[:, 0:1]
            s_ref[g, h] = decay * s_old + lax.dot_general(k_dec, v, tn_dims, preferred_element_type=F32)
            ms = jnp.mean(o * o, axis=-1, keepdims=True)
            o_n = (o * lax.rsqrt(ms + EPS)) * norm_ref[...]
            o_ref[g, :, h * dvh:(h + 1) * dvh] = (o_n * _silu(r)).astype(o_ref.dtype)

    @pl.when(step == pl.num_programs(1) - 1)
    def _():
        sout_ref[...] = s_ref[...]


def _gla(proj, wgu, bias, out_norm, state, *, batch, seq_len, dk, dv):
    m, width = proj.shape
    c = min(CHUNK, seq_len)
    nc = seq_len // c
    n_seq = 2 if batch % 2 == 0 else 1
    dkh, dvh = dk // GLA_HEADS, dv // GLA_HEADS
    rank = wgu.shape[0]
    assert rank <= LANES and 2 * dk + 2 * dv + LANES <= width
    wgu_pad = jnp.pad(wgu, ((0, LANES - rank), (0, 0))).astype(BF16)
    state_spec = pl.BlockSpec((n_seq, GLA_HEADS, dkh, dvh), lambda b, t: (b, 0, 0, 0))
    o, s_new = pl.pallas_call(
        functools.partial(_gla_body, dk=dk, dv=dv),
        grid=(batch // n_seq, nc),
        in_specs=[pl.BlockSpec((n_seq, c, width), lambda b, t: (b, t, 0)),
                  pl.BlockSpec(wgu_pad.shape, lambda b, t: (0, 0)),
                  pl.BlockSpec((1, dk), lambda b, t: (0, 0)),
                  pl.BlockSpec((1, dvh), lambda b, t: (0, 0)),
                  state_spec],
        out_specs=[pl.BlockSpec((n_seq, c, dv), lambda b, t: (b, t, 0)), state_spec],
        out_shape=[jax.ShapeDtypeStruct((batch, seq_len, dv), BF16),
                   jax.ShapeDtypeStruct((batch, GLA_HEADS, dkh, dvh), F32)],
        scratch_shapes=[pltpu.VMEM((n_seq, GLA_HEADS, dkh, dvh), F32)],
        compiler_params=_params(2),
        name="gla",
    )(proj.reshape(batch, seq_len, width), wgu_pad, bias.reshape(1, dk), out_norm.reshape(1, dvh),
      state)
    return o.reshape(m, dv), s_new


def _ffn_body(x_ref, gain_ref, wg_ref, wu_ref, cw_ref, cb_ref, wd_ref, st_ref, o_ref, ns_ref,
              xn_ref, *carry, seq_tile, tiles_per_seq):
    tm = x_ref.shape[0]
    tn = wg_ref.shape[1]
    nseq = tm // seq_tile
    i = pl.program_id(0)
    j = pl.program_id(1)

    @pl.when(j == 0)
    def _():
        x = x_ref[...]
        xn_ref[...] = _rms_to_bf16(x, gain_ref[...])
        o_ref[...] = x

    xn = xn_ref[...]
    g = jnp.dot(xn, wg_ref[...], preferred_element_type=F32)
    u = jnp.dot(xn, wu_ref[...], preferred_element_type=F32)
    if tiles_per_seq > 1:
        prev = jnp.where(i % tiles_per_seq == 0, st_ref[...], carry[0][j])
    else:
        prev = st_ref[...]
    g3 = g.reshape(nseq, seq_tile, tn)
    row = lax.broadcasted_iota(jnp.int32, (nseq, seq_tile, tn), 1)
    back1 = jnp.where(row == 0, prev[:, 1:2, :], pltpu.roll(g, 1, 0).reshape(nseq, seq_tile, tn))
    back2 = jnp.where(row == 0, prev[:, 0:1, :],
                      jnp.where(row == 1, prev[:, 1:2, :],
                                pltpu.roll(g, 2, 0).reshape(nseq, seq_tile, tn)))
    gc = (back2 * cw_ref[0:1, :] + back1 * cw_ref[1:2, :] + g3 * cw_ref[2:3, :]) + cb_ref[...]
    act = (_silu(gc).reshape(tm, tn) * u).astype(BF16)
    o_ref[...] += jnp.dot(act, wd_ref[...], preferred_element_type=F32)
    tail = g3[:, seq_tile - 2:seq_tile, :]
    ns_ref[...] = tail
    if tiles_per_seq > 1:
        carry[0][j] = tail


def _ffn(h, gain, w_up, conv_w, conv_b, w_down, layer, state, *, seq_len, tn=512):
    m, d = h.shape
    dff = w_down.shape[1]
    assert dff % tn == 0
    nj = dff // tn
    tm = _row_tile(m, 1024)
    seq_tile = min(seq_len, tm)
    tiles_per_seq = seq_len // seq_tile
    nseq = tm // seq_tile
    width = conv_w.shape[0]
    assert width == 3 and state.shape[1] == width - 1
    st_map = lambda i, j: (i // tiles_per_seq, 0, j)
    tail_map = lambda i, j: (i, 0, j)
    scratch = [pltpu.VMEM((tm, d), BF16)]
    if tiles_per_seq > 1:
        scratch.append(pltpu.VMEM((nj, nseq, width - 1, tn), F32))
    out, tails = pl.pallas_call(
        functools.partial(_ffn_body, seq_tile=seq_tile, tiles_per_seq=tiles_per_seq),
        grid=(m // tm, nj),
        in_specs=[pl.BlockSpec((tm, d), lambda i, j: (i, 0)),
                  pl.BlockSpec((1, d), lambda i, j: (0, 0)),
                  pl.BlockSpec((None, d, tn), lambda i, j: (layer, 0, j)),
                  pl.BlockSpec((None, d, tn), lambda i, j: (layer, 0, j + nj)),
                  pl.BlockSpec((width, tn), lambda i, j: (0, j)),
                  pl.BlockSpec((1, tn), lambda i, j: (0, j)),
                  pl.BlockSpec((None, tn, d), lambda i, j: (layer, j, 0)),
                  pl.BlockSpec((nseq, width - 1, tn), st_map)],
        out_specs=[pl.BlockSpec((tm, d), lambda i, j: (i, 0)),
                   pl.BlockSpec((nseq, width - 1, tn), tail_map)],
        out_shape=[jax.ShapeDtypeStruct((m, d), F32),
                   jax.ShapeDtypeStruct((m // seq_tile, width - 1, dff), F32)],
        scratch_shapes=scratch,
        compiler_params=_params(2),
        name="conv_ffn",
    )(h, gain.reshape(1, d), w_up, w_up, conv_w, conv_b.reshape(1, dff), w_down, state)
    return out, tails.reshape(-1, tiles_per_seq, width - 1, dff)[:, -1]


def _run_group(h, p, *, batch, seq_len, first_pos, states, weights):
    (cache_k, cache_v, st_conv, st_gla, st_ffn) = states
    w = weights
    m, d = h.shape
    depth = w["norm_mix"].shape[0]
    n_heads = w["a_w_o"].shape[1] // A_HEAD_DIM
    dkv = A_KV_HEADS * A_HEAD_DIM
    cos, sin = _rope_tables(first_pos + jnp.arange(seq_len))
    new_k, new_v, new_conv, new_gla, new_ffn = [], [], [], [], []
    for layer in range(depth):
        kind, slot = layer % 3, layer // 3
        if kind == 0:
            q, k, v = _attn_in(h, w["norm_mix"][layer], w["a_w_qkv"], slot, cos, sin,
                               w["a_q_norm"][slot], w["a_k_norm"][slot], seq_len=seq_len,
                               n_heads=n_heads)
            if cache_k is None:
                o = _attn_prompt(q, k, v, w["a_sinks"][slot], batch=batch, seq_len=seq_len,
                                 n_heads=n_heads)
                new_k.append(k.reshape(batch, seq_len, dkv)[:, seq_len - WINDOW:])
                new_v.append(v.reshape(batch, seq_len, dkv)[:, seq_len - WINDOW:])
            else:
                o, nk, nv = _attn_sample(q, k, v, cache_k[slot], cache_v[slot], w["a_sinks"][slot],
                                         batch=batch, seq_len=seq_len, n_heads=n_heads)
                new_k.append(nk)
                new_v.append(nv)
            mix_in, mix_w = o, w["a_w_o"]
        elif kind == 1:
            tn = 1024
            u = _linear(h, w["b_w_pw1"], slot, n_out=d, tm=1024, tn=tn, epilogue=_ep_glu,
                        out_dtype=F32, gain=w["norm_mix"][layer], w_offsets=(0, d // tn),
                        name="conf_pw1_glu")
            hist = w["b_w_dw"].shape[1] - 1
            if st_conv is None:
                conv_state = jnp.zeros((batch, hist, d), F32)
            else:
                conv_state = st_conv[slot]
            mix_in = _conv_ln(u, conv_state, w["b_w_dw"][slot], w["b_dw_bias"][slot],
                              w["b_ln_g"][slot], w["b_ln_b"][slot], batch=batch, seq_len=seq_len)
            u3 = u.reshape(batch, seq_len, d)
            if seq_len >= hist:
                new_conv.append(u3[:, seq_len - hist:])
            else:
                new_conv.append(jnp.concatenate([conv_state, u3], axis=1)[:, seq_len:])
            mix_w = w["b_w_pw2"]
        else:
            dk = w["c_w_gate_up"].shape[2]
            dv = w["c_w_o"].shape[1]
            proj = _linear(h, w["c_w_in"], slot, n_out=w["c_w_in"].shape[2], tm=1024, tn=GLA_IN_TN,
                           epilogue=_ep_plain, out_dtype=F32, gain=w["norm_mix"][layer],
                           name="gla_in")
            if st_gla is None:
                s0 = jnp.zeros((batch, GLA_HEADS, dk // GLA_HEADS, dv // GLA_HEADS), F32)
            else:
                s0 = st_gla[slot]
            mix_in, s_new = _gla(proj, w["c_w_gate_up"][slot], w["c_gate_bias"][slot],
                                 w["c_out_norm"][slot], s0, batch=batch, seq_len=seq_len, dk=dk, dv=dv)
            new_gla.append(s_new)
            mix_w = w["c_w_o"]
        h = _linear(mix_in, mix_w, slot, n_out=d, tm=512, tn=d, epilogue=_ep_residual,
                    out_dtype=F32, extras=[_residual_extra(h, 512, d)], name="mixer_out")
        dff = w["ffn_w_down"].shape[1]
        if st_ffn is None:
            ffn_state = jnp.zeros((batch, w["ffn_conv_w"].shape[1] - 1, dff), F32)
        else:
            ffn_state = st_ffn[layer]
        h, ns = _ffn(h, w["norm_ffn"][layer], w["ffn_w_up"], w["ffn_conv_w"][layer],
                     w["ffn_conv_b"][layer], w["ffn_w_down"], layer, ffn_state, seq_len=seq_len)
        new_ffn.append(ns)
        tm = _row_tile(m, 512)
        ple_dim = p.shape[-1]
        h = _linear(h, w["ple_w_gate"], layer, n_out=d, tm=512, tn=d, epilogue=_ep_ple,
                    out_dtype=F32, gain=w["ple_norm"][layer],
                    extras=[(p, (None, tm, ple_dim), lambda i, j, layer=layer: (layer, i, 0)),
                            (w["ple_w_proj"], (None, ple_dim, d),
                             lambda i, j, layer=layer: (layer, 0, j))],
                    name="ple")
    kv_shape = (len(new_k), batch, WINDOW, A_KV_HEADS, A_HEAD_DIM)
    return (h.reshape(batch, seq_len, d),
            jnp.stack(new_k).reshape(kv_shape), jnp.stack(new_v).reshape(kv_shape),
            jnp.stack(new_conv), jnp.stack(new_gla), jnp.stack(new_ffn))


def kernel(x_prompt, x_sample, p_prompt, p_sample, cache_k_a, cache_v_a, state_conv_b, state_gla_c, state_ffn_conv, norm_mix, norm_ffn, a_w_qkv, a_q_norm, a_k_norm, a_sinks, a_w_o, b_w_pw1, b_w_dw, b_dw_bias, b_ln_g, b_ln_b, b_w_pw2, c_w_in, c_w_gate_up, c_gate_bias, c_out_norm, c_w_o, ffn_w_up, ffn_conv_w, ffn_conv_b, ffn_w_down, ple_w_proj, ple_norm, ple_w_gate):
    bp, lp, d = x_prompt.shape
    bs, ls, _ = x_sample.shape
    depth = norm_mix.shape[0]
    gla_pad = (-c_w_in.shape[2]) % GLA_IN_TN
    weights = dict(
        norm_mix=norm_mix, norm_ffn=norm_ffn,
        a_w_qkv=a_w_qkv.astype(BF16), a_q_norm=a_q_norm, a_k_norm=a_k_norm, a_sinks=a_sinks,
        a_w_o=a_w_o.astype(BF16),
        b_w_pw1=b_w_pw1.astype(BF16), b_w_dw=b_w_dw, b_dw_bias=b_dw_bias, b_ln_g=b_ln_g,
        b_ln_b=b_ln_b, b_w_pw2=b_w_pw2.astype(BF16),
        c_w_in=jnp.pad(c_w_in, ((0, 0), (0, 0), (0, gla_pad))).astype(BF16),
        c_w_gate_up=c_w_gate_up, c_gate_bias=c_gate_bias, c_out_norm=c_out_norm,
        c_w_o=c_w_o.astype(BF16),
        ffn_w_up=ffn_w_up.astype(BF16), ffn_conv_w=ffn_conv_w, ffn_conv_b=ffn_conv_b,
        ffn_w_down=ffn_w_down.astype(BF16),
        ple_w_proj=ple_w_proj.astype(BF16), ple_norm=ple_norm, ple_w_gate=ple_w_gate.astype(BF16),
    )
    prompt = _run_group(x_prompt.reshape(bp * lp, d), p_prompt.reshape(depth, bp * lp, -1),
                        batch=bp, seq_len=lp, first_pos=0,
                        states=(None, None, None, None, None), weights=weights)
    sample = _run_group(x_sample.reshape(bs * ls, d), p_sample.reshape(depth, bs * ls, -1),
                        batch=bs, seq_len=ls, first_pos=PAST_LEN,
                        states=(cache_k_a, cache_v_a, state_conv_b, state_gla_c, state_ffn_conv),
                        weights=weights)
    return (prompt[0], sample[0]) + prompt[1:] + sample[1:]
```
